```python
import math
import jax, jax.numpy as jnp
from jax import lax
import numpy as np

D_MODEL = 1024
BATCH = 2
SEQ = 8192
DEPTH = 1
DEC_BATCH = 32
DEC_SEQ = 1
PAST_LEN = 8192
PAGE_SIZE = 128

DA_HEADS = 4
DA_QK_DIM = 64
DA_V_DIM = 2 * DA_QK_DIM
FX_HEADS = 8
FX_DIM = 64
DA_Q_COLS = DA_HEADS * 2 * DA_QK_DIM
DA_K_COLS = DA_HEADS * 2 * DA_QK_DIM
DA_V_COLS = DA_HEADS * DA_V_DIM
FX_COLS = FX_HEADS * FX_DIM
D_IN = DA_Q_COLS + DA_K_COLS + DA_V_COLS + 3 * FX_COLS + FX_HEADS
MIX_WIDTH = DA_HEADS * DA_V_DIM + FX_HEADS * FX_DIM
D_FF = 2816
CONV_W = 3
ROPE_THETA = 10000.0
Q_BLOCK = 128
NORM_EPS = 1e-6

kernel_name = "hymba_diff_fox_convffn_step"


def _rmsnorm(x, g):
    x32 = x.astype(jnp.float32)
    y = x32 * lax.rsqrt(jnp.mean(x32 * x32, axis=-1, keepdims=True) + NORM_EPS)
    return (y * g.astype(jnp.float32)).astype(x.dtype)


def _rope(x, pos):
    half = x.shape[-1] // 2
    inv = ROPE_THETA ** (-jnp.arange(half, dtype=jnp.float32) / half)
    ang = pos.astype(jnp.float32)[:, None] * inv[None, :]
    cos = jnp.cos(ang)[:, None, :]
    sin = jnp.sin(ang)[:, None, :]
    x32 = x.astype(jnp.float32)
    x1, x2 = x32[..., :half], x32[..., half:]
    return jnp.concatenate([x1 * cos - x2 * sin, x2 * cos + x1 * sin], axis=-1).astype(x.dtype)


def _project(h, w_in, b_f, pos):
    B, S, _ = h.shape
    z = h @ w_in
    o = 0
    qa = z[..., o:o + DA_Q_COLS].reshape(B, S, DA_HEADS, 2, DA_QK_DIM); o += DA_Q_COLS
    ka = z[..., o:o + DA_K_COLS].reshape(B, S, DA_HEADS, 2, DA_QK_DIM); o += DA_K_COLS
    va = z[..., o:o + DA_V_COLS].reshape(B, S, DA_HEADS, DA_V_DIM); o += DA_V_COLS
    qf = z[..., o:o + FX_COLS].reshape(B, S, FX_HEADS, FX_DIM); o += FX_COLS
    kf = z[..., o:o + FX_COLS].reshape(B, S, FX_HEADS, FX_DIM); o += FX_COLS
    vf = z[..., o:o + FX_COLS].reshape(B, S, FX_HEADS, FX_DIM); o += FX_COLS
    f_logit = z[..., o:o + FX_HEADS]
    q1 = _rope(qa[..., 0, :], pos)
    q2 = _rope(qa[..., 1, :], pos)
    ka_rot = jnp.concatenate([_rope(ka[..., 0, :], pos), _rope(ka[..., 1, :], pos)], axis=-1)
    logf = jax.nn.log_sigmoid(f_logit.astype(jnp.float32) + b_f.astype(jnp.float32))
    return q1, q2, ka_rot, va, qf, kf, vf, logf


def _causal_mask(qpos, kpos):
    return kpos[None, :] <= qpos[:, None]


def _diff_attend(q1, q2, k1, k2, v, qpos, kpos, lam):
    scale = DA_QK_DIM ** -0.5
    mask = _causal_mask(qpos, kpos)

    def probs(q, k):
        s = jnp.einsum('bqhd,bkhd->bhqk', q, k, preferred_element_type=jnp.float32) * scale
        s = jnp.where(mask, s, jnp.finfo(jnp.float32).min)
        return jax.nn.softmax(s, axis=-1)

    a = probs(q1, k1) - lam * probs(q2, k2)
    return jnp.einsum('bhqk,bkhe->bqhe', a.astype(v.dtype), v)


def _fox_attend(q, k, v, cq, ck, qpos, kpos):
    scale = FX_DIM ** -0.5
    mask = _causal_mask(qpos, kpos)
    s = jnp.einsum('bqhd,bkhd->bhqk', q, k, preferred_element_type=jnp.float32) * scale
    bias = jnp.transpose(cq, (0, 2, 1))[..., :, None] - jnp.transpose(ck, (0, 2, 1))[..., None, :]
    s = jnp.where(mask, s + bias, jnp.finfo(jnp.float32).min)
    p = jax.nn.softmax(s, axis=-1)
    return jnp.einsum('bhqk,bkhe->bqhe', p.astype(v.dtype), v)


def _prompt_attention(q1, q2, ka, va, qf, kf, vf, cf, lam):
    B, S = q1.shape[0], q1.shape[1]
    n_blocks = S // Q_BLOCK
    kpos = jnp.arange(S)
    k1, k2 = ka[..., :DA_QK_DIM], ka[..., DA_QK_DIM:]

    def block(i):
        s0 = i * Q_BLOCK
        sl = lambda a: lax.dynamic_slice_in_dim(a, s0, Q_BLOCK, axis=1)
        qpos = s0 + jnp.arange(Q_BLOCK)
        oa = _diff_attend(sl(q1), sl(q2), k1, k2, va, qpos, kpos, lam)
        of = _fox_attend(sl(qf), kf, vf, sl(cf), cf, qpos, kpos)
        return oa, of

    oa, of = lax.map(block, jnp.arange(n_blocks))
    unblock = lambda o: jnp.moveaxis(o, 0, 1).reshape(B, S, o.shape[3], o.shape[4])
    return unblock(oa), unblock(of)


def _merge(oa, of, subln_g, lam_init, w_o):
    B, S = oa.shape[0], oa.shape[1]
    oa = _rmsnorm(oa, subln_g) * (1.0 - lam_init)
    o = jnp.concatenate([oa.reshape(B, S, -1), of.reshape(B, S, -1)], axis=-1)
    return o @ w_o


def _conv_ffn(x, prev, pre_g, w_gate, w_up, conv_w, conv_b, w_down, post_g):
    h = _rmsnorm(x, pre_g)
    g = h @ w_gate
    u = h @ w_up
    S = g.shape[1]
    gp = jnp.concatenate([prev.astype(g.dtype), g], axis=1)
    c = conv_b
    for j in range(CONV_W):
        c = c + conv_w[j] * gp[:, j:j + S]
    a = jax.nn.gelu(c, approximate=True) * u
    out = a @ w_down
    return x + _rmsnorm(out, post_g), gp[:, -(CONV_W - 1):]


def setup_inputs(seed: int = 0) -> dict:
    key = jax.random.key(seed)
    ks = iter(jax.random.split(key, 40))
    f32 = jnp.float32
    nrm = lambda shape, s=1.0: jax.random.normal(next(ks), shape, f32) * s
    n_pages = PAST_LEN // PAGE_SIZE
    n_pool = (DEC_BATCH * n_pages * 5) // 4
    perm = jax.random.permutation(next(ks), n_pool)[:DEC_BATCH * n_pages]
    page_table = perm.reshape(DEC_BATCH, n_pages).astype(jnp.int32)
    return {
        "x_prompt": nrm((BATCH, SEQ, D_MODEL)),
        "x_sample": nrm((DEC_BATCH, DEC_SEQ, D_MODEL)),
        "cache_diff_k": nrm((DEPTH, n_pool, PAGE_SIZE, DA_HEADS, 2 * DA_QK_DIM)),
        "cache_diff_v": nrm((DEPTH, n_pool, PAGE_SIZE, DA_HEADS, DA_V_DIM)),
        "cache_fox_k": nrm((DEPTH, n_pool, PAGE_SIZE, FX_HEADS, FX_DIM)),
        "cache_fox_v": nrm((DEPTH, n_pool, PAGE_SIZE, FX_HEADS, FX_DIM)),
        "cache_fox_logf": jax.nn.log_sigmoid(3.0 + nrm((DEPTH, n_pool, PAGE_SIZE, FX_HEADS))),
        "state_ffn_conv": nrm((DEPTH, DEC_BATCH, CONV_W - 1, D_FF)),
        "page_table": page_table,
        "attn_pre_g": 1.0 + nrm((DEPTH, D_MODEL), 0.02),
        "w_in": nrm((DEPTH, D_MODEL, D_IN), D_MODEL ** -0.5),
        "b_f": 3.0 + nrm((DEPTH, FX_HEADS), 0.1),
        "lam_q1": nrm((DEPTH, DA_QK_DIM), 0.1),
        "lam_k1": nrm((DEPTH, DA_QK_DIM), 0.1),
        "lam_q2": nrm((DEPTH, DA_QK_DIM), 0.1),
        "lam_k2": nrm((DEPTH, DA_QK_DIM), 0.1),
        "subln_g": 1.0 + nrm((DEPTH, DA_V_DIM), 0.02),
        "w_o": nrm((DEPTH, MIX_WIDTH, D_MODEL), MIX_WIDTH ** -0.5),
        "attn_post_g": 1.0 + nrm((DEPTH, D_MODEL), 0.02),
        "ffn_pre_g": 1.0 + nrm((DEPTH, D_MODEL), 0.02),
        "w_gate": nrm((DEPTH, D_MODEL, D_FF), D_MODEL ** -0.5),
        "w_up": nrm((DEPTH, D_MODEL, D_FF), D_MODEL ** -0.5),
        "conv_w": nrm((DEPTH, CONV_W, D_FF), CONV_W ** -0.5),
        "conv_b": nrm((DEPTH, D_FF), 0.02),
        "w_down": nrm((DEPTH, D_FF, D_MODEL), D_FF ** -0.5),
        "ffn_post_g": 1.0 + nrm((DEPTH, D_MODEL), 0.02),
    }


def reference(x_prompt, x_sample, cache_diff_k, cache_diff_v, cache_fox_k, cache_fox_v,
              cache_fox_logf, state_ffn_conv, page_table, attn_pre_g, w_in, b_f,
              lam_q1, lam_k1, lam_q2, lam_k2, subln_g, w_o, attn_post_g, ffn_pre_g,
              w_gate, w_up, conv_w, conv_b, w_down, ffn_post_g):
    f32 = jnp.float32
    B, S = x_prompt.shape[0], x_prompt.shape[1]
    DB, DS = x_sample.shape[0], x_sample.shape[1]
    past_len = page_table.shape[1] * PAGE_SIZE
    pos_p = jnp.arange(S)
    pos_s = past_len + jnp.arange(DS)
    kpos_s = jnp.arange(past_len + DS)

    xp, xs = x_prompt, x_sample
    pk_a, pv_a, pk_f, pv_f, plf, pconv = [], [], [], [], [], []
    sk_a, sv_a, sk_f, sv_f, slf, sconv = [], [], [], [], [], []

    for l in range(DEPTH):
        lam_init = 0.8 - 0.6 * math.exp(-0.3 * l)
        lam = (jnp.exp(jnp.sum(lam_q1[l].astype(f32) * lam_k1[l].astype(f32)))
               - jnp.exp(jnp.sum(lam_q2[l].astype(f32) * lam_k2[l].astype(f32))) + lam_init)

        h = _rmsnorm(xp, attn_pre_g[l])
        q1, q2, ka, va, qf, kf, vf, logf = _project(h, w_in[l], b_f[l], pos_p)
        cf = jnp.cumsum(logf, axis=1)
        oa, of = _prompt_attention(q1, q2, ka, va, qf, kf, vf, cf, lam)
        xp = xp + _rmsnorm(_merge(oa, of, subln_g[l], lam_init, w_o[l]), attn_post_g[l])
        prev0 = jnp.zeros((B, CONV_W - 1, D_FF), xp.dtype)
        xp, conv_p = _conv_ffn(xp, prev0, ffn_pre_g[l], w_gate[l], w_up[l], conv_w[l],
                               conv_b[l], w_down[l], ffn_post_g[l])
        pk_a.append(ka); pv_a.append(va); pk_f.append(kf); pv_f.append(vf)
        plf.append(logf); pconv.append(conv_p)

        h = _rmsnorm(xs, attn_pre_g[l])
        q1s, q2s, kas, vas, qfs, kfs, vfs, logfs = _project(h, w_in[l], b_f[l], pos_s)
        gather = lambda c: c[l][page_table].reshape((DB, past_len) + c.shape[3:])
        ka_all = jnp.concatenate([gather(cache_diff_k).astype(kas.dtype), kas], axis=1)
        va_all = jnp.concatenate([gather(cache_diff_v).astype(vas.dtype), vas], axis=1)
        kf_all = jnp.concatenate([gather(cache_fox_k).astype(kfs.dtype), kfs], axis=1)
        vf_all = jnp.concatenate([gather(cache_fox_v).astype(vfs.dtype), vfs], axis=1)
        lf_all = jnp.concatenate([gather(cache_fox_logf).astype(f32), logfs], axis=1)
        cf_all = jnp.cumsum(lf_all, axis=1)
        oa_s = _diff_attend(q1s, q2s, ka_all[..., :DA_QK_DIM], ka_all[..., DA_QK_DIM:],
                            va_all, pos_s, kpos_s, lam)
        of_s = _fox_attend(qfs, kf_all, vf_all, cf_all[:, past_len:], cf_all, pos_s, kpos_s)
        xs = xs + _rmsnorm(_merge(oa_s, of_s, subln_g[l], lam_init, w_o[l]), attn_post_g[l])
        xs, conv_s = _conv_ffn(xs, state_ffn_conv[l], ffn_pre_g[l], w_gate[l], w_up[l],
                               conv_w[l], conv_b[l], w_down[l], ffn_post_g[l])
        sk_a.append(kas); sv_a.append(vas); sk_f.append(kfs); sv_f.append(vfs)
        slf.append(logfs); sconv.append(conv_s)

    return (xp, xs,
            jnp.stack(pk_a), jnp.stack(pv_a), jnp.stack(pk_f), jnp.stack(pv_f),
            jnp.stack(plf), jnp.stack(pconv),
            jnp.stack(sk_a), jnp.stack(sv_a), jnp.stack(sk_f), jnp.stack(sv_f),
            jnp.stack(slf), jnp.stack(sconv))
```

```python
import functools
import math

import jax
import jax.numpy as jnp
from jax import lax
from jax.experimental import pallas as pl
from jax.experimental.pallas import tpu as pltpu

F32 = jnp.float32
BF16 = jnp.bfloat16

DA_HEADS = 4
DA_QK_DIM = 64
DA_V_DIM = 128
FX_HEADS = 8
FX_DIM = 64
GROUP_COLS = 512
CONV_W = 3
ROPE_THETA = 10000.0
NORM_EPS = 1e-6
PAGE_SIZE = 128
LAM_INIT = 0.8 - 0.6 * math.exp(-0.3 * 0)
QK_SCALE = 0.125

LANES = 128
N_PAIRS = 8
NEG_BIG = -1e30

PROJ_T = 512
ATT_T = 512
FFN_T = 256
FF_CHUNK = 1408
CUM_T = 512
PAGES_PER_STEP = 8
VMEM_LIMIT = 56 * 1024 * 1024


def _rms(x, g):
    return x * lax.rsqrt(jnp.mean(x * x, axis=-1, keepdims=True) + NORM_EPS) * g


def _dot(a, b):
    return jnp.dot(a, b, preferred_element_type=F32)


def _split3(x):
    a1 = x.astype(BF16)
    r1 = x - a1.astype(F32)
    a2 = r1.astype(BF16)
    a3 = (r1 - a2.astype(F32)).astype(BF16)
    return a1, a2, a3


def _tri_cumsum(tri, x):
    a1, a2, a3 = _split3(x)
    return (_dot(tri, a1) + _dot(tri, a2)) + _dot(tri, a3)


def _log_sigmoid(z):
    return jnp.minimum(z, 0.0) - jnp.log1p(jnp.exp(-jnp.abs(z)))


def _rope_tables(pos, inv):
    ang = pos * inv
    return jnp.cos(ang), jnp.sin(ang)


def _rope_chunk(z, cos, sin_signed, first_half):
    swapped = jnp.where(first_half, pltpu.roll(z, LANES - 32, 1), pltpu.roll(z, 32, 1))
    return z * cos + swapped * sin_signed


def _proj_prompt_kernel(x_ref, g_ref, wq_ref, wk_ref, wv_ref, wf_ref, bf_ref, inv_ref, tri_ref,
                        ka_ref, va_ref, kfT_ref, vfT_ref, lfT_ref, katt_ref, qT_ref, vT_ref, cf_ref, cfT_ref,
                        carry_ref, cosb_ref, sinb_ref, *, T):
    b = pl.program_id(0)
    s = pl.program_id(1)
    lane = lax.broadcasted_iota(jnp.int32, (T, LANES), 1)
    first_half = (lane % 64) < 32

    @pl.when((b == 0) & (s == 0))
    def _():
        row = lax.broadcasted_iota(jnp.int32, (T, LANES), 0).astype(F32)
        cosb, sinb = _rope_tables(row, inv_ref[...])
        cosb_ref[...] = cosb
        sinb_ref[...] = sinb

    @pl.when(s == 0)
    def _():
        carry_ref[...] = jnp.zeros_like(carry_ref)

    base = jnp.full((8, LANES), s * T, jnp.int32).astype(F32)
    ca, sa = _rope_tables(base, inv_ref[...])
    ca, sa = ca[0:1], sa[0:1]
    cosb, sinb = cosb_ref[...], sinb_ref[...]
    cos = ca * cosb - sa * sinb
    sin = sa * cosb + ca * sinb
    sin_signed = jnp.where(first_half, -sin, sin)

    h = _rms(x_ref[0], g_ref[...]).astype(BF16)

    zq = _dot(h, wq_ref[...])
    for p in range(4):
        q = _rope_chunk(zq[:, p * LANES:(p + 1) * LANES], cos, sin_signed, first_half)
        qT_ref[0, p] = (q * QK_SCALE).T.astype(BF16)
    for p in range(4):
        q = zq[:, GROUP_COLS + p * LANES:GROUP_COLS + (p + 1) * LANES]
        qT_ref[0, 4 + p] = (q * QK_SCALE).T.astype(BF16)

    zk = _dot(h, wk_ref[...])
    for p in range(4):
        k = _rope_chunk(zk[:, p * LANES:(p + 1) * LANES], cos, sin_signed, first_half)
        ka_ref[0, pl.ds(p, T, stride=DA_HEADS), :] = k
        katt_ref[0, p] = k.astype(BF16)
    for p in range(4):
        kf = zk[:, GROUP_COLS + p * LANES:GROUP_COLS + (p + 1) * LANES]
        kfT_ref[0, p * LANES:(p + 1) * LANES, :] = kf.T
        katt_ref[0, 4 + p] = kf.astype(BF16)

    zv = _dot(h, wv_ref[...])
    for p in range(4):
        v = zv[:, p * LANES:(p + 1) * LANES]
        va_ref[0, pl.ds(p, T, stride=DA_HEADS), :] = v
        vT_ref[0, p] = v.T.astype(BF16)
    for p in range(4):
        vT = zv[:, GROUP_COLS + p * LANES:GROUP_COLS + (p + 1) * LANES].T
        vfT_ref[0, p * LANES:(p + 1) * LANES, :] = vT
        vT_ref[0, 4 + p] = vT.astype(BF16)

    zf = _dot(h, wf_ref[...]) + bf_ref[...]
    lf = jnp.where(lane < FX_HEADS, _log_sigmoid(zf), 0.0)
    lfT_ref[0] = lf.T[:FX_HEADS, :]
    cs = _tri_cumsum(tri_ref[...], lf) + carry_ref[0:1, :]
    carry_ref[0:1, :] = cs[T - 1:T, :]
    cf_ref[0] = cs[:, :FX_HEADS]
    cfT_ref[0] = cs.T[:FX_HEADS, :]


def _const_spec(shape):
    n = len(shape)
    return pl.BlockSpec(shape, lambda *_: (0,) * n, pipeline_mode=pl.Buffered(1))


def _proj_prompt(x, g, wq, wk, wv, wf, bf, inv, tri):
    B, S, D = x.shape
    T = PROJ_T
    tok = lambda w: pl.BlockSpec((1, T, w), lambda b, s: (b, s, 0))
    rows4 = pl.BlockSpec((1, T * DA_HEADS, LANES), lambda b, s: (b, s, 0))
    featT = lambda n: pl.BlockSpec((1, n, T), lambda b, s: (b, 0, s))
    out_shape = (
        jax.ShapeDtypeStruct((B, S * DA_HEADS, LANES), F32),
        jax.ShapeDtypeStruct((B, S * DA_HEADS, LANES), F32),
        jax.ShapeDtypeStruct((B, GROUP_COLS, S), F32),
        jax.ShapeDtypeStruct((B, GROUP_COLS, S), F32),
        jax.ShapeDtypeStruct((B, FX_HEADS, S), F32),
        jax.ShapeDtypeStruct((B, N_PAIRS, S, LANES), BF16),
        jax.ShapeDtypeStruct((B, N_PAIRS, LANES, S), BF16),
        jax.ShapeDtypeStruct((B, N_PAIRS, LANES, S), BF16),
        jax.ShapeDtypeStruct((B, S, FX_HEADS), F32),
        jax.ShapeDtypeStruct((B, FX_HEADS, S), F32),
    )
    out_specs = (
        rows4, rows4, featT(GROUP_COLS), featT(GROUP_COLS), featT(FX_HEADS),
        pl.BlockSpec((1, N_PAIRS, T, LANES), lambda b, s: (b, 0, s, 0)),
        pl.BlockSpec((1, N_PAIRS, LANES, T), lambda b, s: (b, 0, 0, s)),
        pl.BlockSpec((1, N_PAIRS, LANES, T), lambda b, s: (b, 0, 0, s)),
        tok(FX_HEADS),
        featT(FX_HEADS),
    )
    in_specs = [tok(D), _const_spec(g.shape), _const_spec(wq.shape), _const_spec(wk.shape),
                _const_spec(wv.shape), _const_spec(wf.shape), _const_spec(bf.shape),
                _const_spec(inv.shape), _const_spec(tri.shape)]
    return pl.pallas_call(
        functools.partial(_proj_prompt_kernel, T=T),
        grid=(B, S // T),
        in_specs=in_specs,
        out_specs=out_specs,
        out_shape=out_shape,
        scratch_shapes=[pltpu.VMEM((8, LANES), F32), pltpu.VMEM((T, LANES), F32), pltpu.VMEM((T, LANES), F32)],
        compiler_params=pltpu.CompilerParams(dimension_semantics=("arbitrary", "arbitrary"),
                                             vmem_limit_bytes=VMEM_LIMIT),
        name="proj_prompt",
    )(x, g, wq, wk, wv, wf, bf, inv, tri)


def _lambda_value(lamv_ref):
    v = lamv_ref[...]
    d1 = jnp.sum(v[0:1] * v[1:2], axis=1, keepdims=True)
    d2 = jnp.sum(v[2:3] * v[3:4], axis=1, keepdims=True)
    return jnp.exp(d1) - jnp.exp(d2) + LAM_INIT


def _flash_kernel(qT_ref, k_ref, vT_ref, cq_ref, ck_ref, lamv_ref, gcol_ref, o_ref,
                  m_ref, l_ref, acc_ref, ckb_ref, *, T):
    qi = pl.program_id(1)
    ki = pl.program_id(2)

    @pl.when(ki == 0)
    def _():
        m_ref[...] = jnp.full_like(m_ref, NEG_BIG)
        l_ref[...] = jnp.zeros_like(l_ref)
        acc_ref[...] = jnp.zeros_like(acc_ref)

    def step(masked):
        if masked:
            key_i = lax.broadcasted_iota(jnp.int32, (T, T), 0)
            qry_i = lax.broadcasted_iota(jnp.int32, (T, T), 1)
            valid = key_i <= qry_i
        for hh in range(FX_HEADS):
            ckb_ref[hh] = jnp.broadcast_to(ck_ref[0, :, hh:hh + 1], (T, LANES))
        zeros = jnp.zeros((64, T), BF16)

        def weights(qTp, w):
            if w == 0:
                return jnp.concatenate([qTp[:64], zeros], axis=0)
            return jnp.concatenate([zeros, qTp[64:]], axis=0)

        def online(i, s):
            if masked:
                s = jnp.where(valid, s, NEG_BIG)
            m_old = m_ref[i]
            m_new = jnp.maximum(m_old, jnp.max(s, axis=0, keepdims=True))
            alpha = jnp.exp(m_old - m_new)
            p = jnp.exp(s - m_new)
            l_ref[i] = alpha * l_ref[i] + jnp.sum(p, axis=0, keepdims=True)
            m_ref[i] = m_new
            return alpha, p.astype(BF16)

        def diff_body(h, carry):
            kp, qTp, vTp = k_ref[0, h], qT_ref[0, h], vT_ref[0, h]
            for w in range(2):
                i = 2 * h + w
                alpha, p = online(i, _dot(kp, weights(qTp, w)))
                acc_ref[i] = alpha * acc_ref[i] + _dot(vTp, p)
            return carry

        lax.fori_loop(0, DA_HEADS, diff_body, 0)

        def fox_body(g, carry):
            kp, qTp, vTp = k_ref[0, 4 + g], qT_ref[0, 4 + g], vT_ref[0, 4 + g]
            for w in range(2):
                hh = 2 * g + w
                ck = ckb_ref[hh]
                bias = cq_ref[0, hh] - jnp.concatenate([ck] * (T // LANES), axis=1)
                alpha, p = online(8 + hh, _dot(kp, weights(qTp, w)) + bias)
                rows = pl.ds(w * 64, 64)
                acc_ref[8 + g, rows, :] = alpha * acc_ref[8 + g, rows, :] + _dot(vTp[w * 64:(w + 1) * 64], p)
            return carry

        lax.fori_loop(0, FX_HEADS // 2, fox_body, 0)

    @pl.when(ki < qi)
    def _():
        step(False)

    @pl.when(ki == qi)
    def _():
        step(True)
        lam = _lambda_value(lamv_ref)
        for h in range(DA_HEADS):
            o1 = acc_ref[2 * h] * (1.0 / l_ref[2 * h])
            o2 = acc_ref[2 * h + 1] * (1.0 / l_ref[2 * h + 1])
            o = o1 - lam * o2
            ms = jnp.mean(o * o, axis=0, keepdims=True)
            y = o * lax.rsqrt(ms + NORM_EPS) * gcol_ref[...] * (1.0 - LAM_INIT)
            o_ref[0, :, h * LANES:(h + 1) * LANES] = y.T.astype(o_ref.dtype)
        for g in range(FX_HEADS // 2):
            inv_a = jnp.broadcast_to(1.0 / l_ref[8 + 2 * g], (64, T))
            inv_b = jnp.broadcast_to(1.0 / l_ref[8 + 2 * g + 1], (64, T))
            y = acc_ref[8 + g] * jnp.concatenate([inv_a, inv_b], axis=0)
            o_ref[0, :, GROUP_COLS + g * LANES:GROUP_COLS + (g + 1) * LANES] = y.T.astype(o_ref.dtype)


def _flash_prompt(qT, katt, vT, cq, ck, lamv, gcol):
    B, _, _, S = qT.shape
    T = ATT_T
    n = S // T
    kv = lambda qi, ki: jnp.minimum(ki, qi)
    in_specs = [
        pl.BlockSpec((1, N_PAIRS, LANES, T), lambda b, qi, ki: (b, 0, 0, qi)),
        pl.BlockSpec((1, N_PAIRS, T, LANES), lambda b, qi, ki: (b, 0, kv(qi, ki), 0)),
        pl.BlockSpec((1, N_PAIRS, LANES, T), lambda b, qi, ki: (b, 0, 0, kv(qi, ki))),
        pl.BlockSpec((1, FX_HEADS, 1, T), lambda b, qi, ki: (b, 0, 0, qi)),
        pl.BlockSpec((1, T, FX_HEADS), lambda b, qi, ki: (b, kv(qi, ki), 0)),
        pl.BlockSpec(lamv.shape, lambda b, qi, ki: (0, 0)),
        pl.BlockSpec(gcol.shape, lambda b, qi, ki: (0, 0)),
    ]
    return pl.pallas_call(
        functools.partial(_flash_kernel, T=T),
        grid=(B, n, n),
        in_specs=in_specs,
        out_specs=pl.BlockSpec((1, T, 2 * GROUP_COLS), lambda b, qi, ki: (b, qi, 0)),
        out_shape=jax.ShapeDtypeStruct((B, S, 2 * GROUP_COLS), BF16),
        scratch_shapes=[
            pltpu.VMEM((16, 1, T), F32),
            pltpu.VMEM((16, 1, T), F32),
            pltpu.VMEM((12, LANES, T), F32),
            pltpu.VMEM((FX_HEADS, T, LANES), F32),
        ],
        compiler_params=pltpu.CompilerParams(dimension_semantics=("arbitrary", "arbitrary", "arbitrary"),
                                             vmem_limit_bytes=VMEM_LIMIT),
        name="flash_prompt",
    )(qT, katt, vT, cq, ck, lamv, gcol)


def _gated_chunk(h, g, gm1, gm2, wu_ref, wd_ref, cw_ref, cb_ref, c0):
    cols = pl.ds(c0, FF_CHUNK)
    u = _dot(h, wu_ref[:, cols])
    cv = cb_ref[:, cols] + cw_ref[0:1, cols] * gm2 + cw_ref[1:2, cols] * gm1 + cw_ref[2:3, cols] * g
    act = jax.nn.gelu(cv, approximate=True) * u
    return _dot(act.astype(BF16), wd_ref[cols, :])


def _post_prompt_kernel(x_ref, o_ref, wo_ref, g1_ref, g2_ref, wg_ref, wu_ref, cw_ref, cb_ref, wd_ref, g3_ref,
                        y_ref, st_ref, gext_ref, *, T):
    s = pl.program_id(1)
    n_ff = wg_ref.shape[1]

    @pl.when(s == 0)
    def _():
        gext_ref[0:8, :] = jnp.zeros((8, n_ff), F32)

    x1 = x_ref[0] + _rms(_dot(o_ref[0], wo_ref[...]), g1_ref[...])
    h = _rms(x1, g2_ref[...]).astype(BF16)
    out = jnp.zeros_like(x1)
    for c0 in range(0, n_ff, FF_CHUNK):
        cols = pl.ds(c0, FF_CHUNK)
        g = _dot(h, wg_ref[:, cols])
        gext_ref[8:T + 8, cols] = g
        gm1 = gext_ref[7:T + 7, cols]
        gm2 = gext_ref[6:T + 6, cols]
        out = out + _gated_chunk(h, g, gm1, gm2, wu_ref, wd_ref, cw_ref, cb_ref, c0)
    last = gext_ref[T + 6:T + 8, :]
    gext_ref[6:8, :] = last
    y_ref[0] = x1 + _rms(out, g3_ref[...])

    @pl.when(s == pl.num_programs(1) - 1)
    def _():
        st_ref[0] = last


def _post_prompt(x, o, wo, g1, g2, wg, wu, cw, cb, wd, g3):
    B, S, D = x.shape
    T = FFN_T
    n_ff = wg.shape[1]
    tok = lambda w: pl.BlockSpec((1, T, w), lambda b, s: (b, s, 0))
    in_specs = [tok(D), tok(D)] + [_const_spec(a.shape) for a in (wo, g1, g2, wg, wu, cw, cb, wd, g3)]
    return pl.pallas_call(
        functools.partial(_post_prompt_kernel, T=T),
        grid=(B, S // T),
        in_specs=in_specs,
        out_specs=(tok(D), pl.BlockSpec((1, CONV_W - 1, n_ff), lambda b, s: (b, 0, 0))),
        out_shape=(jax.ShapeDtypeStruct((B, S, D), F32), jax.ShapeDtypeStruct((B, CONV_W - 1, n_ff), F32)),
        scratch_shapes=[pltpu.VMEM((T + 8, n_ff), F32)],
        compiler_params=pltpu.CompilerParams(dimension_semantics=("arbitrary", "arbitrary"),
                                             vmem_limit_bytes=VMEM_LIMIT),
        name="post_prompt",
    )(x, o, wo, g1, g2, wg, wu, cw, cb, wd, g3)


def _proj_sample_kernel(x_ref, g_ref, wq_ref, wk_ref, wv_ref, wf_ref, bf_ref, inv_ref, pos_ref,
                        qa_ref, qf_ref, ka_ref, va_ref, kf_ref, vf_ref, lf_ref):
    n = x_ref.shape[0]
    lane = lax.broadcasted_iota(jnp.int32, (n, LANES), 1)
    first_half = (lane % 64) < 32
    cos, sin = _rope_tables(jnp.broadcast_to(pos_ref[...], (n, LANES)), inv_ref[...])
    sin_signed = jnp.where(first_half, -sin, sin)
    h = _rms(x_ref[...], g_ref[...]).astype(BF16)

    zq = _dot(h, wq_ref[...])
    zk = _dot(h, wk_ref[...])
    for p in range(4):
        cols = slice(p * LANES, (p + 1) * LANES)
        qa_ref[:, cols] = _rope_chunk(zq[:, cols], cos, sin_signed, first_half) * QK_SCALE
        ka_ref[:, cols] = _rope_chunk(zk[:, cols], cos, sin_signed, first_half)
    qf_ref[...] = zq[:, GROUP_COLS:] * QK_SCALE
    kf_ref[...] = zk[:, GROUP_COLS:]
    zv = _dot(h, wv_ref[...])
    va_ref[...] = zv[:, :GROUP_COLS]
    vf_ref[...] = zv[:, GROUP_COLS:]
    lf = _log_sigmoid(_dot(h, wf_ref[...]) + bf_ref[...])
    lf_ref[...] = jnp.where(lane < FX_HEADS, lf, 0.0)


def _proj_sample(x, g, wq, wk, wv, wf, bf, inv, pos):
    n = x.shape[0]
    row = lambda w: jax.ShapeDtypeStruct((n, w), F32)
    return pl.pallas_call(
        _proj_sample_kernel,
        out_shape=(row(GROUP_COLS),) * 6 + (row(LANES),),
        compiler_params=pltpu.CompilerParams(vmem_limit_bytes=VMEM_LIMIT),
        name="proj_sample",
    )(x, g, wq, wk, wv, wf, bf, inv, pos)


def _forget_bias_kernel(pt_ref, *refs, n_pages):
    del pt_ref
    page_refs = refs[:n_pages]
    lfnew_ref, upper_ref, later_ref, out_ref = refs[n_pages:]
    x = jnp.concatenate([r[0] for r in page_refs], axis=0)
    a1, a2, a3 = _split3(x)
    within = (_dot(a1, upper_ref[...]) + _dot(a2, upper_ref[...])) + _dot(a3, upper_ref[...])
    page_total = jnp.broadcast_to(within[:, PAGE_SIZE - 1:PAGE_SIZE], x.shape)
    t1, t2, t3 = _split3(page_total)
    suffix = (_dot(later_ref[...], t1) + _dot(later_ref[...], t2)) + _dot(later_ref[...], t3)
    lf_new = jnp.concatenate([lfnew_ref[0]] * n_pages, axis=0)
    out_ref[0] = (suffix + lf_new) - within


def _forget_bias(page_table, cache_lf, lf_new, upper, later):
    n, n_pages = page_table.shape
    rows = n_pages * FX_HEADS

    def page_spec(i):
        return pl.BlockSpec((1, FX_HEADS, PAGE_SIZE), lambda b, pt: (pt[b, i], 0, 0))

    in_specs = [page_spec(i) for i in range(n_pages)] + [
        pl.BlockSpec((1, FX_HEADS, LANES), lambda b, pt: (b, 0, 0)),
        pl.BlockSpec(upper.shape, lambda b, pt: (0, 0)),
        pl.BlockSpec(later.shape, lambda b, pt: (0, 0)),
    ]
    return pl.pallas_call(
        functools.partial(_forget_bias_kernel, n_pages=n_pages),
        grid_spec=pltpu.PrefetchScalarGridSpec(
            num_scalar_prefetch=1,
            grid=(n,),
            in_specs=in_specs,
            out_specs=pl.BlockSpec((1, rows, PAGE_SIZE), lambda b, pt: (b, 0, 0)),
        ),
        out_shape=jax.ShapeDtypeStruct((n, rows, PAGE_SIZE), F32),
        compiler_params=pltpu.CompilerParams(dimension_semantics=("arbitrary",), vmem_limit_bytes=VMEM_LIMIT),
        name="forget_bias",
    )(page_table, *([cache_lf] * n_pages), lf_new, upper, later)


def _decode_kernel(pt_ref, *refs, pps):
    del pt_ref
    kd_refs, vd_refs = refs[0:pps], refs[pps:2 * pps]
    kf_refs, vf_refs = refs[2 * pps:3 * pps], refs[3 * pps:4 * pps]
    (bias_ref, qa_ref, qf_ref, kan_ref, van_ref, kfn_ref, vfn_ref, lamv_ref,
     oa_ref, of_ref, m_ref, l_ref, accd_ref, accf_ref) = refs[4 * pps:]
    c = pl.program_id(1)
    nt = (((1,), (1,)), ((), ()))

    lane4 = lax.broadcasted_iota(jnp.int32, (DA_HEADS, LANES), 1)
    q4 = qa_ref[0]
    qd = jnp.concatenate([jnp.where(lane4 < DA_QK_DIM, q4, 0.0), jnp.where(lane4 >= DA_QK_DIM, q4, 0.0)], axis=0)
    row = lax.broadcasted_iota(jnp.int32, (FX_HEADS, GROUP_COLS), 0)
    col = lax.broadcasted_iota(jnp.int32, (FX_HEADS, GROUP_COLS), 1)
    own_f = col // FX_DIM == row
    qf = jnp.where(own_f, jnp.broadcast_to(qf_ref[0], (FX_HEADS, GROUP_COLS)), 0.0)
    own_d = col % DA_HEADS == row % DA_HEADS

    @pl.when(c == 0)
    def _():
        m_ref[...] = jnp.full_like(m_ref, NEG_BIG)
        l_ref[...] = jnp.zeros_like(l_ref)
        accd_ref[...] = jnp.zeros_like(accd_ref)
        accf_ref[...] = jnp.zeros_like(accf_ref)

    def online(grp, s):
        m_old = m_ref[grp][:, 0:1]
        m_new = jnp.maximum(m_old, jnp.max(s, axis=1, keepdims=True))
        alpha = jnp.exp(m_old - m_new)
        p = jnp.exp(s - m_new)
        l_new = alpha * l_ref[grp][:, 0:1] + jnp.sum(p, axis=1, keepdims=True)
        m_ref[grp] = jnp.broadcast_to(m_new, (8, LANES))
        l_ref[grp] = jnp.broadcast_to(l_new, (8, LANES))
        return alpha, p.astype(BF16)

    qdb = qd.astype(BF16)
    s = jnp.concatenate(
        [jnp.where(own_d, lax.dot_general(qdb, kd_refs[i][0].astype(BF16), nt, preferred_element_type=F32), NEG_BIG)
         for i in range(pps)], axis=1)
    alpha, p = online(0, s)
    w = GROUP_COLS
    pv = _dot(p[:, 0:w], vd_refs[0][0].astype(BF16))
    for i in range(1, pps):
        pv = pv + _dot(p[:, i * w:(i + 1) * w], vd_refs[i][0].astype(BF16))
    accd_ref[...] = alpha * accd_ref[...] + pv

    qfb = qf.astype(BF16)
    s = jnp.concatenate([_dot(qfb, kf_refs[i][0].astype(BF16)) + bias_ref[0, i] for i in range(pps)], axis=1)
    alpha, p = online(1, s)
    w = PAGE_SIZE
    pv = lax.dot_general(p[:, 0:w], vf_refs[0][0].astype(BF16), nt, preferred_element_type=F32)
    for i in range(1, pps):
        pv = pv + lax.dot_general(p[:, i * w:(i + 1) * w], vf_refs[i][0].astype(BF16), nt,
                                  preferred_element_type=F32)
    accf_ref[...] = alpha * accf_ref[...] + pv

    @pl.when(c == pl.num_programs(1) - 1)
    def _():
        def finish(grp, q, k_new, v_new, acc):
            s = jnp.sum(q * k_new, axis=1, keepdims=True)
            m_old = m_ref[grp][:, 0:1]
            m_new = jnp.maximum(m_old, s)
            alpha = jnp.exp(m_old - m_new)
            p = jnp.exp(s - m_new)
            l_new = alpha * l_ref[grp][:, 0:1] + p
            return (alpha * acc + p * v_new) * (1.0 / l_new)

        twice = lambda a: jnp.concatenate([a, a], axis=0)
        od = finish(0, qd, twice(kan_ref[0]), twice(van_ref[0]), accd_ref[...])
        oa_ref[0] = od[0:DA_HEADS] - _lambda_value(lamv_ref) * od[DA_HEADS:]
        of = finish(1, qf, kfn_ref[0], vfn_ref[0], accf_ref[...])
        of_ref[0] = jnp.sum(jnp.where(own_f, of, 0.0), axis=0, keepdims=True)


def _decode_attention(page_table, cdk, cdv, cfk, cfv, bias, qa, qf, ka_new, va_new, kf_new, vf_new, lamv):
    n, n_pages = page_table.shape
    pps = PAGES_PER_STEP
    steps = n_pages // pps

    def page_spec(i):
        return pl.BlockSpec((1, GROUP_COLS, PAGE_SIZE), lambda b, c, pt: (pt[b, c * pps + i], 0, 0))

    row_spec = pl.BlockSpec((1, 1, GROUP_COLS), lambda b, c, pt: (b, 0, 0))
    head_spec = pl.BlockSpec((1, DA_HEADS, LANES), lambda b, c, pt: (b, 0, 0))
    in_specs = ([page_spec(i) for i in range(pps)] * 4
                + [pl.BlockSpec((1, pps, FX_HEADS, PAGE_SIZE), lambda b, c, pt: (b, c, 0, 0))]
                + [head_spec, row_spec, head_spec, head_spec, row_spec, row_spec]
                + [pl.BlockSpec(lamv.shape, lambda b, c, pt: (0, 0))])
    caches = [cdk] * pps + [cdv] * pps + [cfk] * pps + [cfv] * pps
    return pl.pallas_call(
        functools.partial(_decode_kernel, pps=pps),
        grid_spec=pltpu.PrefetchScalarGridSpec(
            num_scalar_prefetch=1,
            grid=(n, steps),
            in_specs=in_specs,
            out_specs=(head_spec, row_spec),
            scratch_shapes=[pltpu.VMEM((2, 8, LANES), F32), pltpu.VMEM((2, 8, LANES), F32),
                            pltpu.VMEM((8, LANES), F32), pltpu.VMEM((8, GROUP_COLS), F32)],
        ),
        out_shape=(jax.ShapeDtypeStruct((n, DA_HEADS, LANES), F32), jax.ShapeDtypeStruct((n, 1, GROUP_COLS), F32)),
        compiler_params=pltpu.CompilerParams(dimension_semantics=("arbitrary", "arbitrary"),
                                             vmem_limit_bytes=VMEM_LIMIT),
        name="decode_attention",
    )(page_table, *caches, bias, qa, qf, ka_new, va_new, kf_new, vf_new, lamv)


def _post_sample_kernel(x_ref, oa_ref, of_ref, sg_ref, wo_ref, g1_ref, g2_ref, wg_ref, wu_ref, cw_ref, cb_ref,
                        wd_ref, g3_ref, p0_ref, p1_ref, y_ref, gate_ref):
    n_ff = wg_ref.shape[1]
    heads = [_rms(oa_ref[:, h * LANES:(h + 1) * LANES], sg_ref[...]) * (1.0 - LAM_INIT) for h in range(DA_HEADS)]
    o = jnp.concatenate(heads + [of_ref[...]], axis=1).astype(BF16)
    x1 = x_ref[...] + _rms(_dot(o, wo_ref[...]), g1_ref[...])
    h = _rms(x1, g2_ref[...]).astype(BF16)
    out = jnp.zeros_like(x1)
    for c0 in range(0, n_ff, FF_CHUNK):
        cols = pl.ds(c0, FF_CHUNK)
        g = _dot(h, wg_ref[:, cols])
        gate_ref[:, cols] = g
        out = out + _gated_chunk(h, g, p1_ref[:, cols], p0_ref[:, cols], wu_ref, wd_ref, cw_ref, cb_ref, c0)
    y_ref[...] = x1 + _rms(out, g3_ref[...])


def _post_sample(x, oa, of, sg, wo, g1, g2, wg, wu, cw, cb, wd, g3, p0, p1):
    n, d = x.shape
    return pl.pallas_call(
        _post_sample_kernel,
        out_shape=(jax.ShapeDtypeStruct((n, d), F32), jax.ShapeDtypeStruct((n, wg.shape[1]), F32)),
        compiler_params=pltpu.CompilerParams(vmem_limit_bytes=VMEM_LIMIT),
        name="post_sample",
    )(x, oa, of, sg, wo, g1, g2, wg, wu, cw, cb, wd, g3, p0, p1)


def kernel(x_prompt, x_sample, cache_diff_k, cache_diff_v, cache_fox_k, cache_fox_v, cache_fox_logf, state_ffn_conv, page_table, attn_pre_g, w_in, b_f, lam_q1, lam_k1, lam_q2, lam_k2, subln_g, w_o, attn_post_g, ffn_pre_g, w_gate, w_up, conv_w, conv_b, w_down, ffn_post_g):
    B, S, D = x_prompt.shape
    DB, DS, _ = x_sample.shape
    assert DS == 1 and w_in.shape[0] == 1, "one layer, one new token per sample"
    n_pool = cache_diff_k.shape[1]
    n_pages = page_table.shape[1]
    past_len = n_pages * PAGE_SIZE
    l = 0

    w = w_in[l]
    sec = lambda i: w[:, i * GROUP_COLS:(i + 1) * GROUP_COLS]
    wq = jnp.concatenate([sec(0), sec(3)], axis=1).astype(BF16)
    wk = jnp.concatenate([sec(1), sec(4)], axis=1).astype(BF16)
    wv = jnp.concatenate([sec(2), sec(5)], axis=1).astype(BF16)
    wf = jnp.pad(w[:, 6 * GROUP_COLS:], ((0, 0), (0, LANES - FX_HEADS))).astype(BF16)
    bf = jnp.pad(b_f[l], (0, LANES - FX_HEADS)).reshape(1, LANES)
    wo, wg, wu, wd = (a[l].astype(BF16) for a in (w_o, w_gate, w_up, w_down))
    row = lambda a: a[l].reshape(1, -1)
    g_pre, g_post, g_ffn, g_out, sg, cb = (row(a) for a in (attn_pre_g, attn_post_g, ffn_pre_g, ffn_post_g, subln_g, conv_b))
    cw = conv_w[l]
    half = DA_QK_DIM // 2
    inv = ROPE_THETA ** (-jnp.arange(half, dtype=F32) / half)
    inv = jnp.tile(inv, LANES // half).reshape(1, LANES)
    tri = jnp.tril(jnp.ones((CUM_T, CUM_T), BF16))
    lamv = jnp.stack([lam_q1[l], lam_k1[l], lam_q2[l], lam_k2[l]])

    ka4, va4, kfT, vfT, lfT, katt, qT, vT, cf, cfT = _proj_prompt(x_prompt, g_pre, wq, wk, wv, wf, bf, inv, tri)
    o = _flash_prompt(qT, katt, vT, cfT.reshape(B, FX_HEADS, 1, S), cf, lamv, sg.reshape(DA_V_DIM, 1))
    y_prompt, conv_p = _post_prompt(x_prompt, o, wo, g_post, g_ffn, wg, wu, cw, cb, wd, g_out)
    heads_last = lambda t: jnp.transpose(t.reshape(B, FX_HEADS, FX_DIM, S), (0, 3, 1, 2))
    kf, vf, logf = heads_last(kfT), heads_last(vfT), jnp.transpose(lfT, (0, 2, 1))

    pos = jnp.full((1, LANES), past_len, F32)
    qa_s, qf_s, ka_s, va_s, kf_s, vf_s, lf_s = _proj_sample(x_sample[:, 0, :], g_pre, wq, wk, wv, wf, bf, inv, pos)
    rows_d = lambda c: c[l].reshape(n_pool, PAGE_SIZE * DA_HEADS, 2 * DA_QK_DIM)
    rows_f = lambda c: jnp.transpose(c[l], (0, 2, 3, 1)).reshape(n_pool, GROUP_COLS, PAGE_SIZE)
    lf_pages = jnp.transpose(cache_fox_logf[l], (0, 2, 1))
    key_i = jnp.arange(PAGE_SIZE)
    upper = (key_i[:, None] <= key_i[None, :]).astype(BF16)
    row_i = jnp.arange(n_pages * FX_HEADS)
    later = ((row_i[:, None] % FX_HEADS == row_i[None, :] % FX_HEADS)
             & (row_i[None, :] // FX_HEADS >= row_i[:, None] // FX_HEADS)).astype(BF16)
    lf_new = jnp.broadcast_to(lf_s[:, :FX_HEADS, None], (DB, FX_HEADS, LANES))
    bias = _forget_bias(page_table, lf_pages, lf_new, upper, later).reshape(DB, n_pages, FX_HEADS, PAGE_SIZE)
    r3 = lambda a: a.reshape(DB, 1, GROUP_COLS)
    r4 = lambda a: a.reshape(DB, DA_HEADS, LANES)
    oa_s, of_s = _decode_attention(page_table, rows_d(cache_diff_k), rows_d(cache_diff_v), rows_f(cache_fox_k),
                                   rows_f(cache_fox_v), bias, r4(qa_s), r3(qf_s), r4(ka_s), r4(va_s), r3(kf_s),
                                   r3(vf_s), lamv)
    prev = state_ffn_conv[l]
    y_sample, gate_s = _post_sample(x_sample[:, 0, :], oa_s.reshape(DB, GROUP_COLS), of_s[:, 0, :], sg, wo, g_post,
                                    g_ffn, wg, wu, cw, cb, wd, g_out, prev[:, 0, :], prev[:, 1, :])
    conv_s = jnp.stack([prev[:, 1, :], gate_s], axis=1)

    lead = lambda a, shape: a.reshape((1,) + shape)
    return (
        y_prompt,
        y_sample.reshape(DB, DS, D),
        lead(ka4, (B, S, DA_HEADS, 2 * DA_QK_DIM)),
        lead(va4, (B, S, DA_HEADS, DA_V_DIM)),
        lead(kf, (B, S, FX_HEADS, FX_DIM)),
        lead(vf, (B, S, FX_HEADS, FX_DIM)),
        lead(logf, (B, S, FX_HEADS)),
        lead(conv_p, (B, CONV_W - 1, conv_p.shape[-1])),
        lead(ka_s, (DB, DS, DA_HEADS, 2 * DA_QK_DIM)),
        lead(va_s, (DB, DS, DA_HEADS, DA_V_DIM)),
        lead(kf_s, (DB, DS, FX_HEADS, FX_DIM)),
        lead(vf_s, (DB, DS, FX_HEADS, FX_DIM)),
        lead(lf_s[:, :FX_HEADS], (DB, DS, FX_HEADS)),
        lead(conv_s, (DB, CONV_W - 1, conv_s.shape[-1])),
    )
```

```python
import functools
import math

import jax
import jax.numpy as jnp
import numpy as np
from jax import lax
from jax.experimental import pallas as pl
from jax.experimental.pallas import tpu as pltpu

F32 = jnp.float32
BF16 = jnp.bfloat16

DA_HEADS = 4
DA_QK_DIM = 64
DA_V_DIM = 128
FX_HEADS = 8
FX_DIM = 64
GROUP_COLS = 512
CONV_W = 3
ROPE_THETA = 10000.0
NORM_EPS = 1e-6
PAGE_SIZE = 128
LAM_INIT = 0.8 - 0.6 * math.exp(-0.3 * 0)
QK_SCALE = 0.125

LANES = 128
N_PAIRS = 8
NEG_BIG = -1e30
LOG2E = 1.4426950408889634
AUX_ROWS = 16
AUX_KEY = (0, 3)
AUX_ONE = 6
DEN_ROWS = 16

PROJ_T = 512
ATT_T = 512
FFN_T = 512
FF_CHUNK = 1408
CUM_T = 512
PAGES_PER_STEP = 8
VMEM_LIMIT = 56 * 1024 * 1024


def _rms(x, g):
    return x * lax.rsqrt(jnp.mean(x * x, axis=-1, keepdims=True) + NORM_EPS) * g


def _dot(a, b):
    return jnp.dot(a, b, preferred_element_type=F32)


def _split3(x):
    a1 = x.astype(BF16)
    r1 = x - a1.astype(F32)
    a2 = r1.astype(BF16)
    a3 = (r1 - a2.astype(F32)).astype(BF16)
    return a1, a2, a3


def _tri_cumsum(tri, x):
    a1, a2, a3 = _split3(x)
    return (_dot(tri, a1) + _dot(tri, a2)) + _dot(tri, a3)


def _log_sigmoid(z):
    return jnp.minimum(z, 0.0) - jnp.log1p(jnp.exp(-jnp.abs(z)))


def _rope_tables(pos, inv):
    ang = pos * inv
    return jnp.cos(ang), jnp.sin(ang)


def _rope_chunk(z, cos, sin_signed, first_half):
    swapped = jnp.where(first_half, pltpu.roll(z, LANES - 32, 1), pltpu.roll(z, 32, 1))
    return z * cos + swapped * sin_signed


def _proj_prompt_kernel(x_ref, g_ref, wq_ref, wk_ref, wv_ref, wf_ref, bf_ref, inv_ref, tri_ref, sel_ref,
                        ka_ref, va_ref, kfT_ref, vfT_ref, lfT_ref, katt_ref, qT_ref, vT_ref, kx_ref, yq_ref,
                        carry_ref, cosb_ref, sinb_ref, *, T):
    b = pl.program_id(0)
    s = pl.program_id(1)
    lane = lax.broadcasted_iota(jnp.int32, (T, LANES), 1)
    first_half = (lane % 64) < 32

    @pl.when((b == 0) & (s == 0))
    def _():
        row = lax.broadcasted_iota(jnp.int32, (T, LANES), 0).astype(F32)
        cosb, sinb = _rope_tables(row, inv_ref[...])
        cosb_ref[...] = cosb
        sinb_ref[...] = sinb

    @pl.when(s == 0)
    def _():
        carry_ref[...] = jnp.zeros_like(carry_ref)

    base = jnp.full((8, LANES), s * T, jnp.int32).astype(F32)
    ca, sa = _rope_tables(base, inv_ref[...])
    ca, sa = ca[0:1], sa[0:1]
    cosb, sinb = cosb_ref[...], sinb_ref[...]
    cos = ca * cosb - sa * sinb
    sin = sa * cosb + ca * sinb
    sin_signed = jnp.where(first_half, -sin, sin)

    h = _rms(x_ref[0], g_ref[...]).astype(BF16)

    zq = _dot(h, wq_ref[...])
    for p in range(4):
        q = _rope_chunk(zq[:, p * LANES:(p + 1) * LANES], cos, sin_signed, first_half)
        qT_ref[0, p] = (q * (QK_SCALE * LOG2E)).T.astype(BF16)
    for p in range(4):
        q = zq[:, GROUP_COLS + p * LANES:GROUP_COLS + (p + 1) * LANES]
        qT_ref[0, 4 + p] = (q * (QK_SCALE * LOG2E)).T.astype(BF16)

    zk = _dot(h, wk_ref[...])
    for p in range(4):
        k = _rope_chunk(zk[:, p * LANES:(p + 1) * LANES], cos, sin_signed, first_half)
        ka_ref[0, pl.ds(p, T, stride=DA_HEADS), :] = k
        katt_ref[0, p] = k.astype(BF16)
    for p in range(4):
        kf = zk[:, GROUP_COLS + p * LANES:GROUP_COLS + (p + 1) * LANES]
        kfT_ref[0, p * LANES:(p + 1) * LANES, :] = kf.T
        katt_ref[0, 4 + p] = kf.astype(BF16)

    zv = _dot(h, wv_ref[...])
    for p in range(4):
        v = zv[:, p * LANES:(p + 1) * LANES]
        va_ref[0, pl.ds(p, T, stride=DA_HEADS), :] = v
        vT_ref[0, p] = v.T.astype(BF16)
    for p in range(4):
        vT = zv[:, GROUP_COLS + p * LANES:GROUP_COLS + (p + 1) * LANES].T
        vfT_ref[0, p * LANES:(p + 1) * LANES, :] = vT
        vT_ref[0, 4 + p] = vT.astype(BF16)

    zf = _dot(h, wf_ref[...]) + bf_ref[...]
    lf = jnp.where(lane < FX_HEADS, _log_sigmoid(zf), 0.0)
    lfT_ref[0] = lf.T[:FX_HEADS, :]
    cs = _tri_cumsum(tri_ref[...], lf) + carry_ref[0:1, :]
    carry_ref[0:1, :] = cs[T - 1:T, :]

    cs2 = cs * LOG2E
    k1, k2, k3 = _split3(cs2)
    aux = (_dot(k1, sel_ref[0]) + _dot(k2, sel_ref[1])) + _dot(k3, sel_ref[2])
    lane4 = lax.broadcasted_iota(jnp.int32, aux.shape, 1) % LANES
    aux = jnp.where((lane4 >= AUX_ONE) & (lane4 < AUX_ONE + 3), 1.0, aux)
    for g in range(4):
        kx_ref[0, g] = aux[:, g * LANES:(g + 1) * LANES].astype(BF16)
    q1, q2, q3 = (t.astype(F32) for t in _split3(cs2.T))
    r = lax.broadcasted_iota(jnp.int32, (AUX_ROWS, T), 0)
    for hd in range(FX_HEADS):
        key_rows = (r >= AUX_KEY[hd % 2]) & (r < AUX_KEY[hd % 2] + 3)
        blk = jnp.where(key_rows, -1.0, 0.0)
        for t, qt in enumerate((q1, q2, q3)):
            blk = jnp.where(r == AUX_ONE + t, jnp.broadcast_to(qt[hd:hd + 1], (AUX_ROWS, T)), blk)
        yq_ref[0, hd] = blk.astype(BF16)


def _const_spec(shape):
    n = len(shape)
    return pl.BlockSpec(shape, lambda *_: (0,) * n, pipeline_mode=pl.Buffered(1))


def _proj_prompt(x, g, wq, wk, wv, wf, bf, inv, tri, sel):
    B, S, D = x.shape
    T = PROJ_T
    tok = lambda w: pl.BlockSpec((1, T, w), lambda b, s: (b, s, 0))
    rows4 = pl.BlockSpec((1, T * DA_HEADS, LANES), lambda b, s: (b, s, 0))
    featT = lambda n: pl.BlockSpec((1, n, T), lambda b, s: (b, 0, s))
    out_shape = (
        jax.ShapeDtypeStruct((B, S * DA_HEADS, LANES), F32),
        jax.ShapeDtypeStruct((B, S * DA_HEADS, LANES), F32),
        jax.ShapeDtypeStruct((B, GROUP_COLS, S), F32),
        jax.ShapeDtypeStruct((B, GROUP_COLS, S), F32),
        jax.ShapeDtypeStruct((B, FX_HEADS, S), F32),
        jax.ShapeDtypeStruct((B, N_PAIRS, S, LANES), BF16),
        jax.ShapeDtypeStruct((B, N_PAIRS, LANES, S), BF16),
        jax.ShapeDtypeStruct((B, N_PAIRS, LANES, S), BF16),
        jax.ShapeDtypeStruct((B, 4, S, LANES), BF16),
        jax.ShapeDtypeStruct((B, FX_HEADS, AUX_ROWS, S), BF16),
    )
    out_specs = (
        rows4, rows4, featT(GROUP_COLS), featT(GROUP_COLS), featT(FX_HEADS),
        pl.BlockSpec((1, N_PAIRS, T, LANES), lambda b, s: (b, 0, s, 0)),
        pl.BlockSpec((1, N_PAIRS, LANES, T), lambda b, s: (b, 0, 0, s)),
        pl.BlockSpec((1, N_PAIRS, LANES, T), lambda b, s: (b, 0, 0, s)),
        pl.BlockSpec((1, 4, T, LANES), lambda b, s: (b, 0, s, 0)),
        pl.BlockSpec((1, FX_HEADS, AUX_ROWS, T), lambda b, s: (b, 0, 0, s)),
    )
    in_specs = [tok(D), _const_spec(g.shape), _const_spec(wq.shape), _const_spec(wk.shape),
                _const_spec(wv.shape), _const_spec(wf.shape), _const_spec(bf.shape),
                _const_spec(inv.shape), _const_spec(tri.shape), _const_spec(sel.shape)]
    return pl.pallas_call(
        functools.partial(_proj_prompt_kernel, T=T),
        grid=(B, S // T),
        in_specs=in_specs,
        out_specs=out_specs,
        out_shape=out_shape,
        scratch_shapes=[pltpu.VMEM((8, LANES), F32), pltpu.VMEM((T, LANES), F32), pltpu.VMEM((T, LANES), F32)],
        compiler_params=pltpu.CompilerParams(dimension_semantics=("arbitrary", "arbitrary"),
                                             vmem_limit_bytes=VMEM_LIMIT),
        name="proj_prompt",
    )(x, g, wq, wk, wv, wf, bf, inv, tri, sel)


def _lambda_value(lamv_ref):
    v = lamv_ref[...]
    d1 = jnp.sum(v[0:1] * v[1:2], axis=1, keepdims=True)
    d2 = jnp.sum(v[2:3] * v[3:4], axis=1, keepdims=True)
    return jnp.exp(d1) - jnp.exp(d2) + LAM_INIT


def _flash_kernel(qtab_ref, ktab_ref, qT_ref, k_ref, kx_ref, vT_ref, yq_ref, lamv_ref, gcol_ref, o_ref,
                  m_ref, acc_ref, *, T):
    t = pl.program_id(1)
    qi = qtab_ref[t]
    ki = ktab_ref[t]

    @pl.when(ki == 0)
    def _():
        m_ref[...] = jnp.full_like(m_ref, NEG_BIG)
        acc_ref[...] = jnp.zeros_like(acc_ref)

    def step(masked):
        if masked:
            key_i = lax.broadcasted_iota(jnp.int32, (T, T), 0)
            qry_i = lax.broadcasted_iota(jnp.int32, (T, T), 1)
            valid = key_i <= qry_i
        zeros = jnp.zeros((64, T), BF16)
        aux_pad = jnp.zeros((LANES - AUX_ROWS, T), BF16)

        def scores(i):
            pair, w = i // 2, i % 2
            qTp = qT_ref[0, pair]
            wq = jnp.concatenate([qTp[:64], zeros] if w == 0 else [zeros, qTp[64:]], axis=0)
            if pair < DA_HEADS:
                return _dot(k_ref[0, pair], wq)
            lhs = jnp.concatenate([k_ref[0, pair], kx_ref[0, pair - DA_HEADS]], axis=1)
            return _dot(lhs, jnp.concatenate([wq, yq_ref[0, i - 2 * DA_HEADS], aux_pad], axis=0))

        def online(i, s):
            if masked:
                s = jnp.where(valid, s, NEG_BIG)
            m_old = m_ref[i]
            m_new = jnp.maximum(m_old, jnp.max(s, axis=0, keepdims=True))
            m_ref[i] = m_new
            return jnp.exp2(m_old - m_new), jnp.exp2(s - m_new).astype(BF16)

        ones_blk = jnp.where(lax.broadcasted_iota(jnp.int32, (DEN_ROWS, T), 0) == 0, 1.0, 0.0).astype(BF16)

        n_soft = 2 * N_PAIRS
        s_next = scores(0)
        for i in range(n_soft):
            s_cur = s_next
            if i + 1 < n_soft:
                s_next = scores(i + 1)
            alpha, p = online(i, s_cur)
            pair = i // 2
            if pair < DA_HEADS:
                vals = vT_ref[0, pair]
            else:
                vals = vT_ref[0, pair, (i % 2) * FX_DIM:(i % 2 + 1) * FX_DIM, :]
            rows = pl.ds(0, vals.shape[0] + DEN_ROWS)
            acc_ref[i, rows, :] = alpha * acc_ref[i, rows, :] + _dot(jnp.concatenate([vals, ones_blk], axis=0), p)

    @pl.when(ki < qi)
    def _():
        step(False)

    @pl.when(ki == qi)
    def _():
        step(True)

        def normalized(i, n_rows):
            return acc_ref[i, 0:n_rows, :] * (1.0 / acc_ref[i, n_rows:n_rows + 1, :])

        lam = _lambda_value(lamv_ref)
        for h in range(DA_HEADS):
            o = normalized(2 * h, DA_V_DIM) - lam * normalized(2 * h + 1, DA_V_DIM)
            ms = jnp.mean(o * o, axis=0, keepdims=True)
            y = o * lax.rsqrt(ms + NORM_EPS) * gcol_ref[...] * (1.0 - LAM_INIT)
            o_ref[0, :, h * LANES:(h + 1) * LANES] = y.T.astype(o_ref.dtype)
        for g in range(FX_HEADS // 2):
            i0 = 2 * DA_HEADS + 2 * g
            y = jnp.concatenate([normalized(i0, FX_DIM), normalized(i0 + 1, FX_DIM)], axis=0)
            o_ref[0, :, GROUP_COLS + g * LANES:GROUP_COLS + (g + 1) * LANES] = y.T.astype(o_ref.dtype)


def _flash_prompt(qT, katt, kx, vT, yq, lamv, gcol):
    B, _, _, S = qT.shape
    T = ATT_T
    n = S // T
    qtab = jnp.asarray([qi for qi in range(n) for _ in range(qi + 1)], jnp.int32)
    ktab = jnp.asarray([ki for qi in range(n) for ki in range(qi + 1)], jnp.int32)
    q_cols = lambda rows: pl.BlockSpec((1, rows[0], rows[1], T), lambda b, t, qt, kt: (b, 0, 0, qt[t]))
    in_specs = [
        q_cols((N_PAIRS, LANES)),
        pl.BlockSpec((1, N_PAIRS, T, LANES), lambda b, t, qt, kt: (b, 0, kt[t], 0)),
        pl.BlockSpec((1, 4, T, LANES), lambda b, t, qt, kt: (b, 0, kt[t], 0)),
        pl.BlockSpec((1, N_PAIRS, LANES, T), lambda b, t, qt, kt: (b, 0, 0, kt[t])),
        q_cols((FX_HEADS, AUX_ROWS)),
        pl.BlockSpec(lamv.shape, lambda b, t, qt, kt: (0, 0)),
        pl.BlockSpec(gcol.shape, lambda b, t, qt, kt: (0, 0)),
    ]
    return pl.pallas_call(
        functools.partial(_flash_kernel, T=T),
        grid_spec=pltpu.PrefetchScalarGridSpec(
            num_scalar_prefetch=2,
            grid=(B, n * (n + 1) // 2),
            in_specs=in_specs,
            out_specs=pl.BlockSpec((1, T, 2 * GROUP_COLS), lambda b, t, qt, kt: (b, qt[t], 0)),
            scratch_shapes=[
                pltpu.VMEM((2 * N_PAIRS, 1, T), F32),
                pltpu.VMEM((2 * N_PAIRS, DA_V_DIM + DEN_ROWS, T), F32),
            ],
        ),
        out_shape=jax.ShapeDtypeStruct((B, S, 2 * GROUP_COLS), BF16),
        compiler_params=pltpu.CompilerParams(dimension_semantics=("arbitrary", "arbitrary"),
                                             vmem_limit_bytes=VMEM_LIMIT),
        name="flash_prompt",
    )(qtab, ktab, qT, katt, kx, vT, yq, lamv, gcol)


def _gated_chunk(h, g, gm1, gm2, wu_ref, wd_ref, cw_ref, cb_ref, c0):
    cols = pl.ds(c0, FF_CHUNK)
    u = _dot(h, wu_ref[:, cols])
    cv = cb_ref[:, cols] + cw_ref[0:1, cols] * gm2 + cw_ref[1:2, cols] * gm1 + cw_ref[2:3, cols] * g
    act = jax.nn.gelu(cv, approximate=True) * u
    return _dot(act.astype(BF16), wd_ref[cols, :])


def _post_prompt_kernel(x_ref, o_ref, wo_ref, g1_ref, g2_ref, wg_ref, wu_ref, cw_ref, cb_ref, wd_ref, g3_ref,
                        y_ref, st_ref, gext_ref, *, T):
    s = pl.program_id(1)
    n_ff = wg_ref.shape[1]

    @pl.when(s == 0)
    def _():
        gext_ref[0:8, :] = jnp.zeros((8, n_ff), F32)

    x1 = x_ref[0] + _rms(_dot(o_ref[0], wo_ref[...]), g1_ref[...])
    h = _rms(x1, g2_ref[...]).astype(BF16)
    out = jnp.zeros_like(x1)
    for c0 in range(0, n_ff, FF_CHUNK):
        cols = pl.ds(c0, FF_CHUNK)
        g = _dot(h, wg_ref[:, cols])
        gext_ref[8:T + 8, cols] = g
        gm1 = gext_ref[7:T + 7, cols]
        gm2 = gext_ref[6:T + 6, cols]
        out = out + _gated_chunk(h, g, gm1, gm2, wu_ref, wd_ref, cw_ref, cb_ref, c0)
    last = gext_ref[T + 6:T + 8, :]
    gext_ref[6:8, :] = last
    y_ref[0] = x1 + _rms(out, g3_ref[...])

    @pl.when(s == pl.num_programs(1) - 1)
    def _():
        st_ref[0] = last


def _post_prompt(x, o, wo, g1, g2, wg, wu, cw, cb, wd, g3):
    B, S, D = x.shape
    T = FFN_T
    n_ff = wg.shape[1]
    tok = lambda w: pl.BlockSpec((1, T, w), lambda b, s: (b, s, 0))
    in_specs = [tok(D), tok(D)] + [_const_spec(a.shape) for a in (wo, g1, g2, wg, wu, cw, cb, wd, g3)]
    return pl.pallas_call(
        functools.partial(_post_prompt_kernel, T=T),
        grid=(B, S // T),
        in_specs=in_specs,
        out_specs=(tok(D), pl.BlockSpec((1, CONV_W - 1, n_ff), lambda b, s: (b, 0, 0))),
        out_shape=(jax.ShapeDtypeStruct((B, S, D), F32), jax.ShapeDtypeStruct((B, CONV_W - 1, n_ff), F32)),
        scratch_shapes=[pltpu.VMEM((T + 8, n_ff), F32)],
        compiler_params=pltpu.CompilerParams(dimension_semantics=("arbitrary", "arbitrary"),
                                             vmem_limit_bytes=VMEM_LIMIT),
        name="post_prompt",
    )(x, o, wo, g1, g2, wg, wu, cw, cb, wd, g3)


def _proj_sample_kernel(x_ref, g_ref, wq_ref, wk_ref, wv_ref, wf_ref, bf_ref, inv_ref, pos_ref,
                        qa_ref, qf_ref, ka_ref, va_ref, kf_ref, vf_ref, lf_ref):
    n = x_ref.shape[0]
    lane = lax.broadcasted_iota(jnp.int32, (n, LANES), 1)
    first_half = (lane % 64) < 32
    cos, sin = _rope_tables(jnp.broadcast_to(pos_ref[...], (n, LANES)), inv_ref[...])
    sin_signed = jnp.where(first_half, -sin, sin)
    h = _rms(x_ref[...], g_ref[...]).astype(BF16)

    zq = _dot(h, wq_ref[...])
    zk = _dot(h, wk_ref[...])
    for p in range(4):
        cols = slice(p * LANES, (p + 1) * LANES)
        qa_ref[:, cols] = _rope_chunk(zq[:, cols], cos, sin_signed, first_half) * QK_SCALE
        ka_ref[:, cols] = _rope_chunk(zk[:, cols], cos, sin_signed, first_half)
    qf_ref[...] = zq[:, GROUP_COLS:] * QK_SCALE
    kf_ref[...] = zk[:, GROUP_COLS:]
    zv = _dot(h, wv_ref[...])
    va_ref[...] = zv[:, :GROUP_COLS]
    vf_ref[...] = zv[:, GROUP_COLS:]
    lf = _log_sigmoid(_dot(h, wf_ref[...]) + bf_ref[...])
    lf_ref[...] = jnp.where(lane < FX_HEADS, lf, 0.0)


def _proj_sample(x, g, wq, wk, wv, wf, bf, inv, pos):
    n = x.shape[0]
    row = lambda w: jax.ShapeDtypeStruct((n, w), F32)
    return pl.pallas_call(
        _proj_sample_kernel,
        out_shape=(row(GROUP_COLS),) * 6 + (row(LANES),),
        compiler_params=pltpu.CompilerParams(vmem_limit_bytes=VMEM_LIMIT),
        name="proj_sample",
    )(x, g, wq, wk, wv, wf, bf, inv, pos)


def _forget_bias_kernel(pt_ref, *refs, n_pages):
    del pt_ref
    page_refs = refs[:n_pages]
    lfnew_ref, upper_ref, later_ref, out_ref = refs[n_pages:]
    x = jnp.concatenate([r[0] for r in page_refs], axis=0)
    a1, a2, a3 = _split3(x)
    within = (_dot(a1, upper_ref[...]) + _dot(a2, upper_ref[...])) + _dot(a3, upper_ref[...])
    page_total = jnp.broadcast_to(within[:, PAGE_SIZE - 1:PAGE_SIZE], x.shape)
    t1, t2, t3 = _split3(page_total)
    suffix = (_dot(later_ref[...], t1) + _dot(later_ref[...], t2)) + _dot(later_ref[...], t3)
    lf_new = jnp.concatenate([lfnew_ref[0]] * n_pages, axis=0)
    out_ref[0] = (suffix + lf_new) - within


def _forget_bias(page_table, cache_lf, lf_new, upper, later):
    n, n_pages = page_table.shape
    rows = n_pages * FX_HEADS

    def page_spec(i):
        return pl.BlockSpec((1, FX_HEADS, PAGE_SIZE), lambda b, pt: (pt[b, i], 0, 0))

    in_specs = [page_spec(i) for i in range(n_pages)] + [
        pl.BlockSpec((1, FX_HEADS, LANES), lambda b, pt: (b, 0, 0)),
        pl.BlockSpec(upper.shape, lambda b, pt: (0, 0)),
        pl.BlockSpec(later.shape, lambda b, pt: (0, 0)),
    ]
    return pl.pallas_call(
        functools.partial(_forget_bias_kernel, n_pages=n_pages),
        grid_spec=pltpu.PrefetchScalarGridSpec(
            num_scalar_prefetch=1,
            grid=(n,),
            in_specs=in_specs,
            out_specs=pl.BlockSpec((1, rows, PAGE_SIZE), lambda b, pt: (b, 0, 0)),
        ),
        out_shape=jax.ShapeDtypeStruct((n, rows, PAGE_SIZE), F32),
        compiler_params=pltpu.CompilerParams(dimension_semantics=("arbitrary",), vmem_limit_bytes=VMEM_LIMIT),
        name="forget_bias",
    )(page_table, *([cache_lf] * n_pages), lf_new, upper, later)


def _decode_kernel(pt_ref, *refs, pps):
    del pt_ref
    kd_refs, vd_refs = refs[0:pps], refs[pps:2 * pps]
    kf_refs, vf_refs = refs[2 * pps:3 * pps], refs[3 * pps:4 * pps]
    (bias_ref, qa_ref, qf_ref, kan_ref, van_ref, kfn_ref, vfn_ref, lamv_ref,
     oa_ref, of_ref, m_ref, l_ref, accd_ref, accf_ref) = refs[4 * pps:]
    c = pl.program_id(1)
    nt = (((1,), (1,)), ((), ()))

    lane4 = lax.broadcasted_iota(jnp.int32, (DA_HEADS, LANES), 1)
    q4 = qa_ref[0]
    qd = jnp.concatenate([jnp.where(lane4 < DA_QK_DIM, q4, 0.0), jnp.where(lane4 >= DA_QK_DIM, q4, 0.0)], axis=0)
    row = lax.broadcasted_iota(jnp.int32, (FX_HEADS, GROUP_COLS), 0)
    col = lax.broadcasted_iota(jnp.int32, (FX_HEADS, GROUP_COLS), 1)
    own_f = col // FX_DIM == row
    qf = jnp.where(own_f, jnp.broadcast_to(qf_ref[0], (FX_HEADS, GROUP_COLS)), 0.0)
    own_d = col % DA_HEADS == row % DA_HEADS

    @pl.when(c == 0)
    def _():
        m_ref[...] = jnp.full_like(m_ref, NEG_BIG)
        l_ref[...] = jnp.zeros_like(l_ref)
        accd_ref[...] = jnp.zeros_like(accd_ref)
        accf_ref[...] = jnp.zeros_like(accf_ref)

    def online(grp, s):
        m_old = m_ref[grp][:, 0:1]
        m_new = jnp.maximum(m_old, jnp.max(s, axis=1, keepdims=True))
        alpha = jnp.exp(m_old - m_new)
        p = jnp.exp(s - m_new)
        l_new = alpha * l_ref[grp][:, 0:1] + jnp.sum(p, axis=1, keepdims=True)
        m_ref[grp] = jnp.broadcast_to(m_new, (8, LANES))
        l_ref[grp] = jnp.broadcast_to(l_new, (8, LANES))
        return alpha, p.astype(BF16)

    qdb = qd.astype(BF16)
    s = jnp.concatenate(
        [jnp.where(own_d, lax.dot_general(qdb, kd_refs[i][0].astype(BF16), nt, preferred_element_type=F32), NEG_BIG)
         for i in range(pps)], axis=1)
    alpha, p = online(0, s)
    w = GROUP_COLS
    pv = _dot(p[:, 0:w], vd_refs[0][0].astype(BF16))
    for i in range(1, pps):
        pv = pv + _dot(p[:, i * w:(i + 1) * w], vd_refs[i][0].astype(BF16))
    accd_ref[...] = alpha * accd_ref[...] + pv

    qfb = qf.astype(BF16)
    s = jnp.concatenate([_dot(qfb, kf_refs[i][0].astype(BF16)) + bias_ref[0, i] for i in range(pps)], axis=1)
    alpha, p = online(1, s)
    w = PAGE_SIZE
    pv = lax.dot_general(p[:, 0:w], vf_refs[0][0].astype(BF16), nt, preferred_element_type=F32)
    for i in range(1, pps):
        pv = pv + lax.dot_general(p[:, i * w:(i + 1) * w], vf_refs[i][0].astype(BF16), nt,
                                  preferred_element_type=F32)
    accf_ref[...] = alpha * accf_ref[...] + pv

    @pl.when(c == pl.num_programs(1) - 1)
    def _():
        def finish(grp, q, k_new, v_new, acc):
            s = jnp.sum(q * k_new, axis=1, keepdims=True)
            m_old = m_ref[grp][:, 0:1]
            m_new = jnp.maximum(m_old, s)
            alpha = jnp.exp(m_old - m_new)
            p = jnp.exp(s - m_new)
            l_new = alpha * l_ref[grp][:, 0:1] + p
            return (alpha * acc + p * v_new) * (1.0 / l_new)

        twice = lambda a: jnp.concatenate([a, a], axis=0)
        od = finish(0, qd, twice(kan_ref[0]), twice(van_ref[0]), accd_ref[...])
        oa_ref[0] = od[0:DA_HEADS] - _lambda_value(lamv_ref) * od[DA_HEADS:]
        of = finish(1, qf, kfn_ref[0], vfn_ref[0], accf_ref[...])
        of_ref[0] = jnp.sum(jnp.where(own_f, of, 0.0), axis=0, keepdims=True)


def _decode_attention(page_table, cdk, cdv, cfk, cfv, bias, qa, qf, ka_new, va_new, kf_new, vf_new, lamv):
    n, n_pages = page_table.shape
    pps = PAGES_PER_STEP
    steps = n_pages // pps

    def page_spec(i):
        return pl.BlockSpec((1, GROUP_COLS, PAGE_SIZE), lambda b, c, pt: (pt[b, c * pps + i], 0, 0))

    row_spec = pl.BlockSpec((1, 1, GROUP_COLS), lambda b, c, pt: (b, 0, 0))
    head_spec = pl.BlockSpec((1, DA_HEADS, LANES), lambda b, c, pt: (b, 0, 0))
    in_specs = ([page_spec(i) for i in range(pps)] * 4
                + [pl.BlockSpec((1, pps, FX_HEADS, PAGE_SIZE), lambda b, c, pt: (b, c, 0, 0))]
                + [head_spec, row_spec, head_spec, head_spec, row_spec, row_spec]
                + [pl.BlockSpec(lamv.shape, lambda b, c, pt: (0, 0))])
    caches = [cdk] * pps + [cdv] * pps + [cfk] * pps + [cfv] * pps
    return pl.pallas_call(
        functools.partial(_decode_kernel, pps=pps),
        grid_spec=pltpu.PrefetchScalarGridSpec(
            num_scalar_prefetch=1,
            grid=(n, steps),
            in_specs=in_specs,
            out_specs=(head_spec, row_spec),
            scratch_shapes=[pltpu.VMEM((2, 8, LANES), F32), pltpu.VMEM((2, 8, LANES), F32),
                            pltpu.VMEM((8, LANES), F32), pltpu.VMEM((8, GROUP_COLS), F32)],
        ),
        out_shape=(jax.ShapeDtypeStruct((n, DA_HEADS, LANES), F32), jax.ShapeDtypeStruct((n, 1, GROUP_COLS), F32)),
        compiler_params=pltpu.CompilerParams(dimension_semantics=("arbitrary", "arbitrary"),
                                             vmem_limit_bytes=VMEM_LIMIT),
        name="decode_attention",
    )(page_table, *caches, bias, qa, qf, ka_new, va_new, kf_new, vf_new, lamv)


def _post_sample_kernel(x_ref, oa_ref, of_ref, sg_ref, wo_ref, g1_ref, g2_ref, wg_ref, wu_ref, cw_ref, cb_ref,
                        wd_ref, g3_ref, p0_ref, p1_ref, y_ref, gate_ref):
    n_ff = wg_ref.shape[1]
    heads = [_rms(oa_ref[:, h * LANES:(h + 1) * LANES], sg_ref[...]) * (1.0 - LAM_INIT) for h in range(DA_HEADS)]
    o = jnp.concatenate(heads + [of_ref[...]], axis=1).astype(BF16)
    x1 = x_ref[...] + _rms(_dot(o, wo_ref[...]), g1_ref[...])
    h = _rms(x1, g2_ref[...]).astype(BF16)
    out = jnp.zeros_like(x1)
    for c0 in range(0, n_ff, FF_CHUNK):
        cols = pl.ds(c0, FF_CHUNK)
        g = _dot(h, wg_ref[:, cols])
        gate_ref[:, cols] = g
        out = out + _gated_chunk(h, g, p1_ref[:, cols], p0_ref[:, cols], wu_ref, wd_ref, cw_ref, cb_ref, c0)
    y_ref[...] = x1 + _rms(out, g3_ref[...])


def _post_sample(x, oa, of, sg, wo, g1, g2, wg, wu, cw, cb, wd, g3, p0, p1):
    n, d = x.shape
    return pl.pallas_call(
        _post_sample_kernel,
        out_shape=(jax.ShapeDtypeStruct((n, d), F32), jax.ShapeDtypeStruct((n, wg.shape[1]), F32)),
        compiler_params=pltpu.CompilerParams(vmem_limit_bytes=VMEM_LIMIT),
        name="post_sample",
    )(x, oa, of, sg, wo, g1, g2, wg, wu, cw, cb, wd, g3, p0, p1)


def kernel(x_prompt, x_sample, cache_diff_k, cache_diff_v, cache_fox_k, cache_fox_v, cache_fox_logf, state_ffn_conv, page_table, attn_pre_g, w_in, b_f, lam_q1, lam_k1, lam_q2, lam_k2, subln_g, w_o, attn_post_g, ffn_pre_g, w_gate, w_up, conv_w, conv_b, w_down, ffn_post_g):
    B, S, D = x_prompt.shape
    DB, DS, _ = x_sample.shape
    assert DS == 1 and w_in.shape[0] == 1, "one layer, one new token per sample"
    n_pool = cache_diff_k.shape[1]
    n_pages = page_table.shape[1]
    past_len = n_pages * PAGE_SIZE
    l = 0

    w = w_in[l]
    sec = lambda i: w[:, i * GROUP_COLS:(i + 1) * GROUP_COLS]
    wq = jnp.concatenate([sec(0), sec(3)], axis=1).astype(BF16)
    wk = jnp.concatenate([sec(1), sec(4)], axis=1).astype(BF16)
    wv = jnp.concatenate([sec(2), sec(5)], axis=1).astype(BF16)
    wf = jnp.pad(w[:, 6 * GROUP_COLS:], ((0, 0), (0, LANES - FX_HEADS))).astype(BF16)
    bf = jnp.pad(b_f[l], (0, LANES - FX_HEADS)).reshape(1, LANES)
    wo, wg, wu, wd = (a[l].astype(BF16) for a in (w_o, w_gate, w_up, w_down))
    row = lambda a: a[l].reshape(1, -1)
    g_pre, g_post, g_ffn, g_out, sg, cb = (row(a) for a in (attn_pre_g, attn_post_g, ffn_pre_g, ffn_post_g, subln_g, conv_b))
    cw = conv_w[l]
    half = DA_QK_DIM // 2
    inv = ROPE_THETA ** (-jnp.arange(half, dtype=F32) / half)
    inv = jnp.tile(inv, LANES // half).reshape(1, LANES)
    tri = jnp.tril(jnp.ones((CUM_T, CUM_T), BF16))
    lamv = jnp.stack([lam_q1[l], lam_k1[l], lam_q2[l], lam_k2[l]])

    sel = np.zeros((3, LANES, 4 * LANES), np.float32)
    for hd in range(FX_HEADS):
        for term in range(3):
            sel[term, hd, (hd // 2) * LANES + AUX_KEY[hd % 2] + term] = 1.0
    ka4, va4, kfT, vfT, lfT, katt, qT, vT, kx, yq = _proj_prompt(x_prompt, g_pre, wq, wk, wv, wf, bf, inv, tri,
                                                                 jnp.asarray(sel, BF16))
    o = _flash_prompt(qT, katt, kx, vT, yq, lamv, sg.reshape(DA_V_DIM, 1))
    y_prompt, conv_p = _post_prompt(x_prompt, o, wo, g_post, g_ffn, wg, wu, cw, cb, wd, g_out)
    heads_last = lambda t: jnp.transpose(t.reshape(B, FX_HEADS, FX_DIM, S), (0, 3, 1, 2))
    kf, vf, logf = heads_last(kfT), heads_last(vfT), jnp.transpose(lfT, (0, 2, 1))

    pos = jnp.full((1, LANES), past_len, F32)
    qa_s, qf_s, ka_s, va_s, kf_s, vf_s, lf_s = _proj_sample(x_sample[:, 0, :], g_pre, wq, wk, wv, wf, bf, inv, pos)
    rows_d = lambda c: c[l].reshape(n_pool, PAGE_SIZE * DA_HEADS, 2 * DA_QK_DIM)
    rows_f = lambda c: jnp.transpose(c[l], (0, 2, 3, 1)).reshape(n_pool, GROUP_COLS, PAGE_SIZE)
    lf_pages = jnp.transpose(cache_fox_logf[l], (0, 2, 1))
    key_i = jnp.arange(PAGE_SIZE)
    upper = (key_i[:, None] <= key_i[None, :]).astype(BF16)
    row_i = jnp.arange(n_pages * FX_HEADS)
    later = ((row_i[:, None] % FX_HEADS == row_i[None, :] % FX_HEADS)
             & (row_i[None, :] // FX_HEADS >= row_i[:, None] // FX_HEADS)).astype(BF16)
    lf_new = jnp.broadcast_to(lf_s[:, :FX_HEADS, None], (DB, FX_HEADS, LANES))
    bias = _forget_bias(page_table, lf_pages, lf_new, upper, later).reshape(DB, n_pages, FX_HEADS, PAGE_SIZE)
    r3 = lambda a: a.reshape(DB, 1, GROUP_COLS)
    r4 = lambda a: a.reshape(DB, DA_HEADS, LANES)
    oa_s, of_s = _decode_attention(page_table, rows_d(cache_diff_k), rows_d(cache_diff_v), rows_f(cache_fox_k),
                                   rows_f(cache_fox_v), bias, r4(qa_s), r3(qf_s), r4(ka_s), r4(va_s), r3(kf_s),
                                   r3(vf_s), lamv)
    prev = state_ffn_conv[l]
    y_sample, gate_s = _post_sample(x_sample[:, 0, :], oa_s.reshape(DB, GROUP_COLS), of_s[:, 0, :], sg, wo, g_post,
                                    g_ffn, wg, wu, cw, cb, wd, g_out, prev[:, 0, :], prev[:, 1, :])
    conv_s = jnp.stack([prev[:, 1, :], gate_s], axis=1)

    lead = lambda a, shape: a.reshape((1,) + shape)
    return (
        y_prompt,
        y_sample.reshape(DB, DS, D),
        lead(ka4, (B, S, DA_HEADS, 2 * DA_QK_DIM)),
        lead(va4, (B, S, DA_HEADS, DA_V_DIM)),
        lead(kf, (B, S, FX_HEADS, FX_DIM)),
        lead(vf, (B, S, FX_HEADS, FX_DIM)),
        lead(logf, (B, S, FX_HEADS)),
        lead(conv_p, (B, CONV_W - 1, conv_p.shape[-1])),
        lead(ka_s, (DB, DS, DA_HEADS, 2 * DA_QK_DIM)),
        lead(va_s, (DB, DS, DA_HEADS, DA_V_DIM)),
        lead(kf_s, (DB, DS, FX_HEADS, FX_DIM)),
        lead(vf_s, (DB, DS, FX_HEADS, FX_DIM)),
        lead(lf_s[:, :FX_HEADS], (DB, DS, FX_HEADS)),
        lead(conv_s, (DB, CONV_W - 1, conv_s.shape[-1])),
    )
```

```python
import functools
import math

import jax
import jax.numpy as jnp
import numpy as np
from jax import lax
from jax.experimental import pallas as pl
from jax.experimental.pallas import tpu as pltpu

F32 = jnp.float32
BF16 = jnp.bfloat16

DA_HEADS = 4
DA_QK_DIM = 64
DA_V_DIM = 128
FX_HEADS = 8
FX_DIM = 64
GROUP_COLS = 512
CONV_W = 3
ROPE_THETA = 10000.0
NORM_EPS = 1e-6
PAGE_SIZE = 128
LAM_INIT = 0.8 - 0.6 * math.exp(-0.3 * 0)
QK_SCALE = 0.125

LANES = 128
N_PAIRS = 8
NEG_BIG = -1e30
LOG2E = 1.4426950408889634
AUX_ROWS = 16
AUX_KEY = (0, 3)
AUX_ONE = 6
DEN_ROWS = 16

PROJ_T = 512
ATT_T = 512
FFN_T = 512
FF_CHUNK = 1408
QK_AHEAD = 3
CUM_T = 512
PAGES_PER_STEP = 16
VMEM_LIMIT = 56 * 1024 * 1024


def _rms(x, g):
    return x * lax.rsqrt(jnp.mean(x * x, axis=-1, keepdims=True) + NORM_EPS) * g


def _dot(a, b):
    return jnp.dot(a, b, preferred_element_type=F32)


def _split3(x):
    a1 = x.astype(BF16)
    r1 = x - a1.astype(F32)
    a2 = r1.astype(BF16)
    a3 = (r1 - a2.astype(F32)).astype(BF16)
    return a1, a2, a3


def _tri_cumsum(tri, x):
    a1, a2, a3 = _split3(x)
    return (_dot(tri, a1) + _dot(tri, a2)) + _dot(tri, a3)


def _log_sigmoid(z):
    return jnp.minimum(z, 0.0) - jnp.log1p(jnp.exp(-jnp.abs(z)))


def _rope_tables(pos, inv):
    ang = pos * inv
    return jnp.cos(ang), jnp.sin(ang)


def _rope_chunk(z, cos, sin_signed, first_half):
    swapped = jnp.where(first_half, pltpu.roll(z, LANES - 32, 1), pltpu.roll(z, 32, 1))
    return z * cos + swapped * sin_signed


def _proj_prompt_kernel(x_ref, g_ref, wq_ref, wk_ref, wv_ref, wf_ref, bf_ref, inv_ref, tri_ref, sel_ref,
                        ka_ref, va_ref, kfT_ref, vfT_ref, lfT_ref, katt_ref, qT_ref, vT_ref, kx_ref, yq_ref,
                        carry_ref, cosb_ref, sinb_ref, *, T):
    b = pl.program_id(0)
    s = pl.program_id(1)
    lane = lax.broadcasted_iota(jnp.int32, (T, LANES), 1)
    first_half = (lane % 64) < 32

    @pl.when((b == 0) & (s == 0))
    def _():
        row = lax.broadcasted_iota(jnp.int32, (T, LANES), 0).astype(F32)
        cosb, sinb = _rope_tables(row, inv_ref[...])
        cosb_ref[...] = cosb
        sinb_ref[...] = sinb

    @pl.when(s == 0)
    def _():
        carry_ref[...] = jnp.zeros_like(carry_ref)

    base = jnp.full((8, LANES), s * T, jnp.int32).astype(F32)
    ca, sa = _rope_tables(base, inv_ref[...])
    ca, sa = ca[0:1], sa[0:1]
    cosb, sinb = cosb_ref[...], sinb_ref[...]
    cos = ca * cosb - sa * sinb
    sin = sa * cosb + ca * sinb
    sin_signed = jnp.where(first_half, -sin, sin)

    h = _rms(x_ref[0], g_ref[...]).astype(BF16)

    zq = _dot(h, wq_ref[...])
    for p in range(4):
        q = _rope_chunk(zq[:, p * LANES:(p + 1) * LANES], cos, sin_signed, first_half)
        qT_ref[0, p] = (q * (QK_SCALE * LOG2E)).T.astype(BF16)
    for p in range(4):
        q = zq[:, GROUP_COLS + p * LANES:GROUP_COLS + (p + 1) * LANES]
        qT_ref[0, 4 + p] = (q * (QK_SCALE * LOG2E)).T.astype(BF16)

    zk = _dot(h, wk_ref[...])
    for p in range(4):
        k = _rope_chunk(zk[:, p * LANES:(p + 1) * LANES], cos, sin_signed, first_half)
        ka_ref[0, pl.ds(p, T, stride=DA_HEADS), :] = k
        katt_ref[0, p] = k.astype(BF16)
    for p in range(4):
        kf = zk[:, GROUP_COLS + p * LANES:GROUP_COLS + (p + 1) * LANES]
        kfT_ref[0, p * LANES:(p + 1) * LANES, :] = kf.T
        katt_ref[0, 4 + p] = kf.astype(BF16)

    zv = _dot(h, wv_ref[...])
    for p in range(4):
        v = zv[:, p * LANES:(p + 1) * LANES]
        va_ref[0, pl.ds(p, T, stride=DA_HEADS), :] = v
        vT_ref[0, p] = v.T.astype(BF16)
    for p in range(4):
        vT = zv[:, GROUP_COLS + p * LANES:GROUP_COLS + (p + 1) * LANES].T
        vfT_ref[0, p * LANES:(p + 1) * LANES, :] = vT
        vT_ref[0, 4 + p] = vT.astype(BF16)

    zf = _dot(h, wf_ref[...]) + bf_ref[...]
    lf = jnp.where(lane < FX_HEADS, _log_sigmoid(zf), 0.0)
    lfT_ref[0] = lf.T[:FX_HEADS, :]
    cs = _tri_cumsum(tri_ref[...], lf) + carry_ref[0:1, :]
    carry_ref[0:1, :] = cs[T - 1:T, :]

    cs2 = cs * LOG2E
    k1, k2, k3 = _split3(cs2)
    aux = (_dot(k1, sel_ref[0]) + _dot(k2, sel_ref[1])) + _dot(k3, sel_ref[2])
    lane4 = lax.broadcasted_iota(jnp.int32, aux.shape, 1) % LANES
    aux = jnp.where((lane4 >= AUX_ONE) & (lane4 < AUX_ONE + 3), 1.0, aux)
    for g in range(4):
        kx_ref[0, g] = aux[:, g * LANES:(g + 1) * LANES].astype(BF16)
    q1, q2, q3 = (t.astype(F32) for t in _split3(cs2.T))
    r = lax.broadcasted_iota(jnp.int32, (AUX_ROWS, T), 0)
    for hd in range(FX_HEADS):
        key_rows = (r >= AUX_KEY[hd % 2]) & (r < AUX_KEY[hd % 2] + 3)
        blk = jnp.where(key_rows, -1.0, 0.0)
        for t, qt in enumerate((q1, q2, q3)):
            blk = jnp.where(r == AUX_ONE + t, jnp.broadcast_to(qt[hd:hd + 1], (AUX_ROWS, T)), blk)
        yq_ref[0, hd] = blk.astype(BF16)


def _const_spec(shape):
    n = len(shape)
    return pl.BlockSpec(shape, lambda *_: (0,) * n, pipeline_mode=pl.Buffered(1))


def _proj_prompt(x, g, wq, wk, wv, wf, bf, inv, tri, sel):
    B, S, D = x.shape
    T = PROJ_T
    tok = lambda w: pl.BlockSpec((1, T, w), lambda b, s: (b, s, 0))
    rows4 = pl.BlockSpec((1, T * DA_HEADS, LANES), lambda b, s: (b, s, 0))
    featT = lambda n: pl.BlockSpec((1, n, T), lambda b, s: (b, 0, s))
    out_shape = (
        jax.ShapeDtypeStruct((B, S * DA_HEADS, LANES), F32),
        jax.ShapeDtypeStruct((B, S * DA_HEADS, LANES), F32),
        jax.ShapeDtypeStruct((B, GROUP_COLS, S), F32),
        jax.ShapeDtypeStruct((B, GROUP_COLS, S), F32),
        jax.ShapeDtypeStruct((B, FX_HEADS, S), F32),
        jax.ShapeDtypeStruct((B, N_PAIRS, S, LANES), BF16),
        jax.ShapeDtypeStruct((B, N_PAIRS, LANES, S), BF16),
        jax.ShapeDtypeStruct((B, N_PAIRS, LANES, S), BF16),
        jax.ShapeDtypeStruct((B, 4, S, LANES), BF16),
        jax.ShapeDtypeStruct((B, FX_HEADS, AUX_ROWS, S), BF16),
    )
    out_specs = (
        rows4, rows4, featT(GROUP_COLS), featT(GROUP_COLS), featT(FX_HEADS),
        pl.BlockSpec((1, N_PAIRS, T, LANES), lambda b, s: (b, 0, s, 0)),
        pl.BlockSpec((1, N_PAIRS, LANES, T), lambda b, s: (b, 0, 0, s)),
        pl.BlockSpec((1, N_PAIRS, LANES, T), lambda b, s: (b, 0, 0, s)),
        pl.BlockSpec((1, 4, T, LANES), lambda b, s: (b, 0, s, 0)),
        pl.BlockSpec((1, FX_HEADS, AUX_ROWS, T), lambda b, s: (b, 0, 0, s)),
    )
    in_specs = [tok(D), _const_spec(g.shape), _const_spec(wq.shape), _const_spec(wk.shape),
                _const_spec(wv.shape), _const_spec(wf.shape), _const_spec(bf.shape),
                _const_spec(inv.shape), _const_spec(tri.shape), _const_spec(sel.shape)]
    return pl.pallas_call(
        functools.partial(_proj_prompt_kernel, T=T),
        grid=(B, S // T),
        in_specs=in_specs,
        out_specs=out_specs,
        out_shape=out_shape,
        scratch_shapes=[pltpu.VMEM((8, LANES), F32), pltpu.VMEM((T, LANES), F32), pltpu.VMEM((T, LANES), F32)],
        compiler_params=pltpu.CompilerParams(dimension_semantics=("arbitrary", "arbitrary"),
                                             vmem_limit_bytes=VMEM_LIMIT),
        name="proj_prompt",
    )(x, g, wq, wk, wv, wf, bf, inv, tri, sel)


def _lambda_value(lamv_ref):
    v = lamv_ref[...]
    d1 = jnp.sum(v[0:1] * v[1:2], axis=1, keepdims=True)
    d2 = jnp.sum(v[2:3] * v[3:4], axis=1, keepdims=True)
    return jnp.exp(d1) - jnp.exp(d2) + LAM_INIT


def _flash_step(qi, ki, qT_ref, k_ref, kx_ref, vT_ref, yq_ref, lamv_ref, gcol_ref, o_ref, m_ref, acc_ref, T):
    @pl.when(ki == 0)
    def _():
        m_ref[...] = jnp.full_like(m_ref, NEG_BIG)
        acc_ref[...] = jnp.zeros_like(acc_ref)

    def step(masked):
        if masked:
            key_i = lax.broadcasted_iota(jnp.int32, (T, T), 0)
            qry_i = lax.broadcasted_iota(jnp.int32, (T, T), 1)
            valid = key_i <= qry_i
        zeros = jnp.zeros((64, T), BF16)
        aux_pad = jnp.zeros((LANES - AUX_ROWS, T), BF16)

        def scores(i):
            pair, w = i // 2, i % 2
            qTp = qT_ref[0, pair]
            wq = jnp.concatenate([qTp[:64], zeros] if w == 0 else [zeros, qTp[64:]], axis=0)
            if pair < DA_HEADS:
                return _dot(k_ref[0, pair], wq)
            lhs = jnp.concatenate([k_ref[0, pair], kx_ref[0, pair - DA_HEADS]], axis=1)
            return _dot(lhs, jnp.concatenate([wq, yq_ref[0, i - 2 * DA_HEADS], aux_pad], axis=0))

        def online(i, s):
            if masked:
                s = jnp.where(valid, s, NEG_BIG)
            m_old = m_ref[i]
            m_new = jnp.maximum(m_old, jnp.max(s, axis=0, keepdims=True))
            m_ref[i] = m_new
            return jnp.exp2(m_old - m_new), jnp.exp2(s - m_new).astype(BF16)

        ones_blk = jnp.where(lax.broadcasted_iota(jnp.int32, (DEN_ROWS, T), 0) == 0, 1.0, 0.0).astype(BF16)

        n_soft = 2 * N_PAIRS
        ahead = [scores(j) for j in range(QK_AHEAD)]
        for i in range(n_soft):
            if i + QK_AHEAD < n_soft:
                ahead.append(scores(i + QK_AHEAD))
            alpha, p = online(i, ahead.pop(0))
            pair = i // 2
            if pair < DA_HEADS:
                vals = vT_ref[0, pair]
            else:
                vals = vT_ref[0, pair, (i % 2) * FX_DIM:(i % 2 + 1) * FX_DIM, :]
            rows = pl.ds(0, vals.shape[0] + DEN_ROWS)
            acc_ref[i, rows, :] = alpha * acc_ref[i, rows, :] + _dot(jnp.concatenate([vals, ones_blk], axis=0), p)

    @pl.when(ki < qi)
    def _():
        step(False)

    @pl.when(ki == qi)
    def _():
        step(True)

        def normalized(i, n_rows):
            return acc_ref[i, 0:n_rows, :] * (1.0 / acc_ref[i, n_rows:n_rows + 1, :])

        lam = _lambda_value(lamv_ref)
        for h in range(DA_HEADS):
            o = normalized(2 * h, DA_V_DIM) - lam * normalized(2 * h + 1, DA_V_DIM)
            ms = jnp.mean(o * o, axis=0, keepdims=True)
            y = o * lax.rsqrt(ms + NORM_EPS) * gcol_ref[...] * (1.0 - LAM_INIT)
            o_ref[0, :, h * LANES:(h + 1) * LANES] = y.T.astype(o_ref.dtype)
        for g in range(FX_HEADS // 2):
            i0 = 2 * DA_HEADS + 2 * g
            y = jnp.concatenate([normalized(i0, FX_DIM), normalized(i0 + 1, FX_DIM)], axis=0)
            o_ref[0, :, GROUP_COLS + g * LANES:GROUP_COLS + (g + 1) * LANES] = y.T.astype(o_ref.dtype)


def _flash_kernel(qtab_ref, ktab_ref, *refs, T):
    t = pl.program_id(1)
    _flash_step(qtab_ref[t], ktab_ref[t], *refs, T)


def _flash_prompt(qT, katt, kx, vT, yq, lamv, gcol):
    B, _, _, S = qT.shape
    T = ATT_T
    n = S // T
    qtab = jnp.asarray([qi for qi in range(n) for _ in range(qi + 1)], jnp.int32)
    ktab = jnp.asarray([ki for qi in range(n) for ki in range(qi + 1)], jnp.int32)
    q_cols = lambda rows: pl.BlockSpec((1, rows[0], rows[1], T), lambda b, t, qt, kt: (b, 0, 0, qt[t]))
    in_specs = [
        q_cols((N_PAIRS, LANES)),
        pl.BlockSpec((1, N_PAIRS, T, LANES), lambda b, t, qt, kt: (b, 0, kt[t], 0)),
        pl.BlockSpec((1, 4, T, LANES), lambda b, t, qt, kt: (b, 0, kt[t], 0)),
        pl.BlockSpec((1, N_PAIRS, LANES, T), lambda b, t, qt, kt: (b, 0, 0, kt[t])),
        q_cols((FX_HEADS, AUX_ROWS)),
        pl.BlockSpec(lamv.shape, lambda b, t, qt, kt: (0, 0)),
        pl.BlockSpec(gcol.shape, lambda b, t, qt, kt: (0, 0)),
    ]
    return pl.pallas_call(
        functools.partial(_flash_kernel, T=T),
        grid_spec=pltpu.PrefetchScalarGridSpec(
            num_scalar_prefetch=2,
            grid=(B, n * (n + 1) // 2),
            in_specs=in_specs,
            out_specs=pl.BlockSpec((1, T, 2 * GROUP_COLS), lambda b, t, qt, kt: (b, qt[t], 0)),
            scratch_shapes=[
                pltpu.VMEM((2 * N_PAIRS, 1, T), F32),
                pltpu.VMEM((2 * N_PAIRS, DA_V_DIM + DEN_ROWS, T), F32),
            ],
        ),
        out_shape=jax.ShapeDtypeStruct((B, S, 2 * GROUP_COLS), BF16),
        compiler_params=pltpu.CompilerParams(dimension_semantics=("arbitrary", "arbitrary"),
                                             vmem_limit_bytes=VMEM_LIMIT),
        name="flash_prompt",
    )(qtab, ktab, qT, katt, kx, vT, yq, lamv, gcol)


def _gated_chunk(h, g, gm1, gm2, wu_ref, wd_ref, cw_ref, cb_ref, c0):
    cols = pl.ds(c0, FF_CHUNK)
    u = _dot(h, wu_ref[:, cols])
    cv = cb_ref[:, cols] + cw_ref[0:1, cols] * gm2 + cw_ref[1:2, cols] * gm1 + cw_ref[2:3, cols] * g
    act = jax.nn.gelu(cv, approximate=True) * u
    return _dot(act.astype(BF16), wd_ref[cols, :])


def _post_prompt_kernel(x_ref, o_ref, wo_ref, g1_ref, g2_ref, wg_ref, wu_ref, cw_ref, cb_ref, wd_ref, g3_ref,
                        y_ref, st_ref, gext_ref, *, T):
    s = pl.program_id(1)
    n_ff = wg_ref.shape[1]

    @pl.when(s == 0)
    def _():
        gext_ref[0:8, :] = jnp.zeros((8, n_ff), F32)

    x1 = x_ref[0] + _rms(_dot(o_ref[0], wo_ref[...]), g1_ref[...])
    h = _rms(x1, g2_ref[...]).astype(BF16)
    out = jnp.zeros_like(x1)
    for c0 in range(0, n_ff, FF_CHUNK):
        cols = pl.ds(c0, FF_CHUNK)
        g = _dot(h, wg_ref[:, cols])
        gext_ref[8:T + 8, cols] = g
        gm1 = gext_ref[7:T + 7, cols]
        gm2 = gext_ref[6:T + 6, cols]
        out = out + _gated_chunk(h, g, gm1, gm2, wu_ref, wd_ref, cw_ref, cb_ref, c0)
    last = gext_ref[T + 6:T + 8, :]
    gext_ref[6:8, :] = last
    y_ref[0] = x1 + _rms(out, g3_ref[...])

    @pl.when(s == pl.num_programs(1) - 1)
    def _():
        st_ref[0] = last


def _post_prompt(x, o, wo, g1, g2, wg, wu, cw, cb, wd, g3):
    B, S, D = x.shape
    T = FFN_T
    n_ff = wg.shape[1]
    tok = lambda w: pl.BlockSpec((1, T, w), lambda b, s: (b, s, 0))
    in_specs = [tok(D), tok(D)] + [_const_spec(a.shape) for a in (wo, g1, g2, wg, wu, cw, cb, wd, g3)]
    return pl.pallas_call(
        functools.partial(_post_prompt_kernel, T=T),
        grid=(B, S // T),
        in_specs=in_specs,
        out_specs=(tok(D), pl.BlockSpec((1, CONV_W - 1, n_ff), lambda b, s: (b, 0, 0))),
        out_shape=(jax.ShapeDtypeStruct((B, S, D), F32), jax.ShapeDtypeStruct((B, CONV_W - 1, n_ff), F32)),
        scratch_shapes=[pltpu.VMEM((T + 8, n_ff), F32)],
        compiler_params=pltpu.CompilerParams(dimension_semantics=("arbitrary", "arbitrary"),
                                             vmem_limit_bytes=VMEM_LIMIT),
        name="post_prompt",
    )(x, o, wo, g1, g2, wg, wu, cw, cb, wd, g3)


def _proj_sample_kernel(x_ref, g_ref, wq_ref, wk_ref, wv_ref, wf_ref, bf_ref, inv_ref, pos_ref,
                        qa_ref, qf_ref, ka_ref, va_ref, kf_ref, vf_ref, lf_ref):
    n = x_ref.shape[0]
    lane = lax.broadcasted_iota(jnp.int32, (n, LANES), 1)
    first_half = (lane % 64) < 32
    cos, sin = _rope_tables(jnp.broadcast_to(pos_ref[...], (n, LANES)), inv_ref[...])
    sin_signed = jnp.where(first_half, -sin, sin)
    h = _rms(x_ref[...], g_ref[...]).astype(BF16)

    zq = _dot(h, wq_ref[...])
    zk = _dot(h, wk_ref[...])
    for p in range(4):
        cols = slice(p * LANES, (p + 1) * LANES)
        qa_ref[:, cols] = _rope_chunk(zq[:, cols], cos, sin_signed, first_half) * QK_SCALE
        ka_ref[:, cols] = _rope_chunk(zk[:, cols], cos, sin_signed, first_half)
    qf_ref[...] = zq[:, GROUP_COLS:] * QK_SCALE
    kf_ref[...] = zk[:, GROUP_COLS:]
    zv = _dot(h, wv_ref[...])
    va_ref[...] = zv[:, :GROUP_COLS]
    vf_ref[...] = zv[:, GROUP_COLS:]
    lf = _log_sigmoid(_dot(h, wf_ref[...]) + bf_ref[...])
    lf_ref[...] = jnp.where(lane < FX_HEADS, lf, 0.0)


def _proj_sample(x, g, wq, wk, wv, wf, bf, inv, pos):
    n = x.shape[0]
    row = lambda w: jax.ShapeDtypeStruct((n, w), F32)
    return pl.pallas_call(
        _proj_sample_kernel,
        out_shape=(row(GROUP_COLS),) * 6 + (row(LANES),),
        compiler_params=pltpu.CompilerParams(vmem_limit_bytes=VMEM_LIMIT),
        name="proj_sample",
    )(x, g, wq, wk, wv, wf, bf, inv, pos)


def _forget_bias_kernel(pt_ref, *refs, n_pages):
    del pt_ref
    page_refs = refs[:n_pages]
    lfnew_ref, upper_ref, later_ref, out_ref = refs[n_pages:]
    x = jnp.concatenate([r[0] for r in page_refs], axis=0)
    a1, a2, a3 = _split3(x)
    within = (_dot(a1, upper_ref[...]) + _dot(a2, upper_ref[...])) + _dot(a3, upper_ref[...])
    page_total = jnp.broadcast_to(within[:, PAGE_SIZE - 1:PAGE_SIZE], x.shape)
    t1, t2, t3 = _split3(page_total)
    suffix = (_dot(later_ref[...], t1) + _dot(later_ref[...], t2)) + _dot(later_ref[...], t3)
    lf_new = jnp.concatenate([lfnew_ref[0]] * n_pages, axis=0)
    out_ref[0] = (suffix + lf_new) - within


def _forget_bias(page_table, cache_lf, lf_new, upper, later):
    n, n_pages = page_table.shape
    rows = n_pages * FX_HEADS

    def page_spec(i):
        return pl.BlockSpec((1, FX_HEADS, PAGE_SIZE), lambda b, pt: (pt[b, i], 0, 0))

    in_specs = [page_spec(i) for i in range(n_pages)] + [
        pl.BlockSpec((1, FX_HEADS, LANES), lambda b, pt: (b, 0, 0)),
        pl.BlockSpec(upper.shape, lambda b, pt: (0, 0)),
        pl.BlockSpec(later.shape, lambda b, pt: (0, 0)),
    ]
    return pl.pallas_call(
        functools.partial(_forget_bias_kernel, n_pages=n_pages),
        grid_spec=pltpu.PrefetchScalarGridSpec(
            num_scalar_prefetch=1,
            grid=(n,),
            in_specs=in_specs,
            out_specs=pl.BlockSpec((1, rows, PAGE_SIZE), lambda b, pt: (b, 0, 0)),
        ),
        out_shape=jax.ShapeDtypeStruct((n, rows, PAGE_SIZE), F32),
        compiler_params=pltpu.CompilerParams(dimension_semantics=("arbitrary",), vmem_limit_bytes=VMEM_LIMIT),
        name="forget_bias",
    )(page_table, *([cache_lf] * n_pages), lf_new, upper, later)


def _decode_step(c, n_chunks, refs, oa_ref, of_ref, m_ref, l_ref, accd_ref, accf_ref, pps):
    kd_refs, vd_refs = refs[0:pps], refs[pps:2 * pps]
    kf_refs, vf_refs = refs[2 * pps:3 * pps], refs[3 * pps:4 * pps]
    bias_ref, qa_ref, qf_ref, kan_ref, van_ref, kfn_ref, vfn_ref, lamv_ref = refs[4 * pps:]
    nt = (((1,), (1,)), ((), ()))

    lane4 = lax.broadcasted_iota(jnp.int32, (DA_HEADS, LANES), 1)
    q4 = qa_ref[0]
    qd = jnp.concatenate([jnp.where(lane4 < DA_QK_DIM, q4, 0.0), jnp.where(lane4 >= DA_QK_DIM, q4, 0.0)], axis=0)
    row = lax.broadcasted_iota(jnp.int32, (FX_HEADS, GROUP_COLS), 0)
    col = lax.broadcasted_iota(jnp.int32, (FX_HEADS, GROUP_COLS), 1)
    own_f = col // FX_DIM == row
    qf = jnp.where(own_f, jnp.broadcast_to(qf_ref[0], (FX_HEADS, GROUP_COLS)), 0.0)
    own_d = col % DA_HEADS == row % DA_HEADS

    @pl.when(c == 0)
    def _():
        m_ref[...] = jnp.full_like(m_ref, NEG_BIG)
        l_ref[...] = jnp.zeros_like(l_ref)
        accd_ref[...] = jnp.zeros_like(accd_ref)
        accf_ref[...] = jnp.zeros_like(accf_ref)

    def online(grp, s):
        m_old = m_ref[grp][:, 0:1]
        m_new = jnp.maximum(m_old, jnp.max(s, axis=1, keepdims=True))
        alpha = jnp.exp(m_old - m_new)
        p = jnp.exp(s - m_new)
        l_new = alpha * l_ref[grp][:, 0:1] + jnp.sum(p, axis=1, keepdims=True)
        m_ref[grp] = jnp.broadcast_to(m_new, (8, LANES))
        l_ref[grp] = jnp.broadcast_to(l_new, (8, LANES))
        return alpha, p.astype(BF16)

    qdb = qd.astype(BF16)
    s = jnp.concatenate(
        [jnp.where(own_d, lax.dot_general(qdb, kd_refs[i][0].astype(BF16), nt, preferred_element_type=F32), NEG_BIG)
         for i in range(pps)], axis=1)
    alpha, p = online(0, s)
    w = GROUP_COLS
    pv = _dot(p[:, 0:w], vd_refs[0][0].astype(BF16))
    for i in range(1, pps):
        pv = pv + _dot(p[:, i * w:(i + 1) * w], vd_refs[i][0].astype(BF16))
    accd_ref[...] = alpha * accd_ref[...] + pv

    qfb = qf.astype(BF16)
    s = jnp.concatenate([_dot(qfb, kf_refs[i][0].astype(BF16)) + bias_ref[0, i] for i in range(pps)], axis=1)
    alpha, p = online(1, s)
    w = PAGE_SIZE
    pv = lax.dot_general(p[:, 0:w], vf_refs[0][0].astype(BF16), nt, preferred_element_type=F32)
    for i in range(1, pps):
        pv = pv + lax.dot_general(p[:, i * w:(i + 1) * w], vf_refs[i][0].astype(BF16), nt,
                                  preferred_element_type=F32)
    accf_ref[...] = alpha * accf_ref[...] + pv

    @pl.when(c == n_chunks - 1)
    def _():
        def finish(grp, q, k_new, v_new, acc):
            s = jnp.sum(q * k_new, axis=1, keepdims=True)
            m_old = m_ref[grp][:, 0:1]
            m_new = jnp.maximum(m_old, s)
            alpha = jnp.exp(m_old - m_new)
            p = jnp.exp(s - m_new)
            l_new = alpha * l_ref[grp][:, 0:1] + p
            return (alpha * acc + p * v_new) * (1.0 / l_new)

        twice = lambda a: jnp.concatenate([a, a], axis=0)
        od = finish(0, qd, twice(kan_ref[0]), twice(van_ref[0]), accd_ref[...])
        oa_ref[0] = od[0:DA_HEADS] - _lambda_value(lamv_ref) * od[DA_HEADS:]
        of = finish(1, qf, kfn_ref[0], vfn_ref[0], accf_ref[...])
        of_ref[0] = jnp.sum(jnp.where(own_f, of, 0.0), axis=0, keepdims=True)


def _decode_kernel(pt_ref, *refs, pps):
    del pt_ref
    n_in = 4 * pps + 8
    _decode_step(pl.program_id(1), pl.num_programs(1), refs[:n_in], *refs[n_in:], pps)


def _decode_attention(page_table, cdk, cdv, cfk, cfv, bias, qa, qf, ka_new, va_new, kf_new, vf_new, lamv):
    n, n_pages = page_table.shape
    pps = PAGES_PER_STEP
    steps = n_pages // pps

    def page_spec(i):
        return pl.BlockSpec((1, GROUP_COLS, PAGE_SIZE), lambda b, c, pt: (pt[b, c * pps + i], 0, 0))

    row_spec = pl.BlockSpec((1, 1, GROUP_COLS), lambda b, c, pt: (b, 0, 0))
    head_spec = pl.BlockSpec((1, DA_HEADS, LANES), lambda b, c, pt: (b, 0, 0))
    in_specs = ([page_spec(i) for i in range(pps)] * 4
                + [pl.BlockSpec((1, pps, FX_HEADS, PAGE_SIZE), lambda b, c, pt: (b, c, 0, 0))]
                + [head_spec, row_spec, head_spec, head_spec, row_spec, row_spec]
                + [pl.BlockSpec(lamv.shape, lambda b, c, pt: (0, 0))])
    caches = [cdk] * pps + [cdv] * pps + [cfk] * pps + [cfv] * pps
    return pl.pallas_call(
        functools.partial(_decode_kernel, pps=pps),
        grid_spec=pltpu.PrefetchScalarGridSpec(
            num_scalar_prefetch=1,
            grid=(n, steps),
            in_specs=in_specs,
            out_specs=(head_spec, row_spec),
            scratch_shapes=[pltpu.VMEM((2, 8, LANES), F32), pltpu.VMEM((2, 8, LANES), F32),
                            pltpu.VMEM((8, LANES), F32), pltpu.VMEM((8, GROUP_COLS), F32)],
        ),
        out_shape=(jax.ShapeDtypeStruct((n, DA_HEADS, LANES), F32), jax.ShapeDtypeStruct((n, 1, GROUP_COLS), F32)),
        compiler_params=pltpu.CompilerParams(dimension_semantics=("arbitrary", "arbitrary"),
                                             vmem_limit_bytes=VMEM_LIMIT),
        name="decode_attention",
    )(page_table, *caches, bias, qa, qf, ka_new, va_new, kf_new, vf_new, lamv)


def _post_sample_kernel(x_ref, oa_ref, of_ref, sg_ref, wo_ref, g1_ref, g2_ref, wg_ref, wu_ref, cw_ref, cb_ref,
                        wd_ref, g3_ref, p0_ref, p1_ref, y_ref, gate_ref):
    n_ff = wg_ref.shape[1]
    heads = [_rms(oa_ref[:, h * LANES:(h + 1) * LANES], sg_ref[...]) * (1.0 - LAM_INIT) for h in range(DA_HEADS)]
    o = jnp.concatenate(heads + [of_ref[...]], axis=1).astype(BF16)
    x1 = x_ref[...] + _rms(_dot(o, wo_ref[...]), g1_ref[...])
    h = _rms(x1, g2_ref[...]).astype(BF16)
    out = jnp.zeros_like(x1)
    for c0 in range(0, n_ff, FF_CHUNK):
        cols = pl.ds(c0, FF_CHUNK)
        g = _dot(h, wg_ref[:, cols])
        gate_ref[:, cols] = g
        out = out + _gated_chunk(h, g, p1_ref[:, cols], p0_ref[:, cols], wu_ref, wd_ref, cw_ref, cb_ref, c0)
    y_ref[...] = x1 + _rms(out, g3_ref[...])


def _post_sample(x, oa, of, sg, wo, g1, g2, wg, wu, cw, cb, wd, g3, p0, p1):
    n, d = x.shape
    return pl.pallas_call(
        _post_sample_kernel,
        out_shape=(jax.ShapeDtypeStruct((n, d), F32), jax.ShapeDtypeStruct((n, wg.shape[1]), F32)),
        compiler_params=pltpu.CompilerParams(vmem_limit_bytes=VMEM_LIMIT),
        name="post_sample",
    )(x, oa, of, sg, wo, g1, g2, wg, wu, cw, cb, wd, g3, p0, p1)


def kernel(x_prompt, x_sample, cache_diff_k, cache_diff_v, cache_fox_k, cache_fox_v, cache_fox_logf, state_ffn_conv, page_table, attn_pre_g, w_in, b_f, lam_q1, lam_k1, lam_q2, lam_k2, subln_g, w_o, attn_post_g, ffn_pre_g, w_gate, w_up, conv_w, conv_b, w_down, ffn_post_g):
    B, S, D = x_prompt.shape
    DB, DS, _ = x_sample.shape
    assert DS == 1 and w_in.shape[0] == 1, "one layer, one new token per sample"
    n_pool = cache_diff_k.shape[1]
    n_pages = page_table.shape[1]
    past_len = n_pages * PAGE_SIZE
    l = 0

    w = w_in[l]
    sec = lambda i: w[:, i * GROUP_COLS:(i + 1) * GROUP_COLS]
    wq = jnp.concatenate([sec(0), sec(3)], axis=1).astype(BF16)
    wk = jnp.concatenate([sec(1), sec(4)], axis=1).astype(BF16)
    wv = jnp.concatenate([sec(2), sec(5)], axis=1).astype(BF16)
    wf = jnp.pad(w[:, 6 * GROUP_COLS:], ((0, 0), (0, LANES - FX_HEADS))).astype(BF16)
    bf = jnp.pad(b_f[l], (0, LANES - FX_HEADS)).reshape(1, LANES)
    wo, wg, wu, wd = (a[l].astype(BF16) for a in (w_o, w_gate, w_up, w_down))
    row = lambda a: a[l].reshape(1, -1)
    g_pre, g_post, g_ffn, g_out, sg, cb = (row(a) for a in (attn_pre_g, attn_post_g, ffn_pre_g, ffn_post_g, subln_g, conv_b))
    cw = conv_w[l]
    half = DA_QK_DIM // 2
    inv = ROPE_THETA ** (-jnp.arange(half, dtype=F32) / half)
    inv = jnp.tile(inv, LANES // half).reshape(1, LANES)
    tri = jnp.tril(jnp.ones((CUM_T, CUM_T), BF16))
    lamv = jnp.stack([lam_q1[l], lam_k1[l], lam_q2[l], lam_k2[l]])

    sel = np.zeros((3, LANES, 4 * LANES), np.float32)
    for hd in range(FX_HEADS):
        for term in range(3):
            sel[term, hd, (hd // 2) * LANES + AUX_KEY[hd % 2] + term] = 1.0
    ka4, va4, kfT, vfT, lfT, katt, qT, vT, kx, yq = _proj_prompt(x_prompt, g_pre, wq, wk, wv, wf, bf, inv, tri,
                                                                 jnp.asarray(sel, BF16))
    heads_last = lambda t: jnp.transpose(t.reshape(B, FX_HEADS, FX_DIM, S), (0, 3, 1, 2))
    kf, vf, logf = heads_last(kfT), heads_last(vfT), jnp.transpose(lfT, (0, 2, 1))

    pos = jnp.full((1, LANES), past_len, F32)
    qa_s, qf_s, ka_s, va_s, kf_s, vf_s, lf_s = _proj_sample(x_sample[:, 0, :], g_pre, wq, wk, wv, wf, bf, inv, pos)
    rows_d = lambda c: c[l].reshape(n_pool, PAGE_SIZE * DA_HEADS, 2 * DA_QK_DIM)
    rows_f = lambda c: jnp.transpose(c[l], (0, 2, 3, 1)).reshape(n_pool, GROUP_COLS, PAGE_SIZE)
    lf_pages = jnp.transpose(cache_fox_logf[l], (0, 2, 1))
    key_i = jnp.arange(PAGE_SIZE)
    upper = (key_i[:, None] <= key_i[None, :]).astype(BF16)
    row_i = jnp.arange(n_pages * FX_HEADS)
    later = ((row_i[:, None] % FX_HEADS == row_i[None, :] % FX_HEADS)
             & (row_i[None, :] // FX_HEADS >= row_i[:, None] // FX_HEADS)).astype(BF16)
    lf_new = jnp.broadcast_to(lf_s[:, :FX_HEADS, None], (DB, FX_HEADS, LANES))
    bias = _forget_bias(page_table, lf_pages, lf_new, upper, later).reshape(DB, n_pages, FX_HEADS, PAGE_SIZE)
    r3 = lambda a: a.reshape(DB, 1, GROUP_COLS)
    r4 = lambda a: a.reshape(DB, DA_HEADS, LANES)
    oa_s, of_s = _decode_attention(page_table, rows_d(cache_diff_k), rows_d(cache_diff_v), rows_f(cache_fox_k),
                                   rows_f(cache_fox_v), bias, r4(qa_s), r3(qf_s), r4(ka_s), r4(va_s), r3(kf_s),
                                   r3(vf_s), lamv)
    o = _flash_prompt(qT, katt, kx, vT, yq, lamv, sg.reshape(DA_V_DIM, 1))
    y_prompt, conv_p = _post_prompt(x_prompt, o, wo, g_post, g_ffn, wg, wu, cw, cb, wd, g_out)
    prev = state_ffn_conv[l]
    y_sample, gate_s = _post_sample(x_sample[:, 0, :], oa_s.reshape(DB, GROUP_COLS), of_s[:, 0, :], sg, wo, g_post,
                                    g_ffn, wg, wu, cw, cb, wd, g_out, prev[:, 0, :], prev[:, 1, :])
    conv_s = jnp.stack([prev[:, 1, :], gate_s], axis=1)

    lead = lambda a, shape: a.reshape((1,) + shape)
    return (
        y_prompt,
        y_sample.reshape(DB, DS, D),
        lead(ka4, (B, S, DA_HEADS, 2 * DA_QK_DIM)),
        lead(va4, (B, S, DA_HEADS, DA_V_DIM)),
        lead(kf, (B, S, FX_HEADS, FX_DIM)),
        lead(vf, (B, S, FX_HEADS, FX_DIM)),
        lead(logf, (B, S, FX_HEADS)),
        lead(conv_p, (B, CONV_W - 1, conv_p.shape[-1])),
        lead(ka_s, (DB, DS, DA_HEADS, 2 * DA_QK_DIM)),
        lead(va_s, (DB, DS, DA_HEADS, DA_V_DIM)),
        lead(kf_s, (DB, DS, FX_HEADS, FX_DIM)),
        lead(vf_s, (DB, DS, FX_HEADS, FX_DIM)),
        lead(lf_s[:, :FX_HEADS], (DB, DS, FX_HEADS)),
        lead(conv_s, (DB, CONV_W - 1, conv_s.shape[-1])),
    )
```

```python
import functools
import math

import jax
import jax.numpy as jnp
import numpy as np
from jax import lax
from jax.experimental import pallas as pl
from jax.experimental.pallas import tpu as pltpu

F32 = jnp.float32
BF16 = jnp.bfloat16

DA_HEADS = 4
DA_QK_DIM = 64
DA_V_DIM = 128
FX_HEADS = 8
FX_DIM = 64
GROUP_COLS = 512
CONV_W = 3
ROPE_THETA = 10000.0
NORM_EPS = 1e-6
PAGE_SIZE = 128
LAM_INIT = 0.8 - 0.6 * math.exp(-0.3 * 0)
QK_SCALE = 0.125

LANES = 128
N_PAIRS = 8
NEG_BIG = -1e30
LOG2E = 1.4426950408889634
AUX_ROWS = 16
AUX_KEY = (0, 3)
AUX_ONE = 6
DEN_ROWS = 16

PROJ_T = 512
ATT_T = 512
FFN_T = 512
FF_CHUNK = 1408
QK_AHEAD = 3
CUM_T = 512
PAGES_PER_STEP = 16
VMEM_LIMIT = 56 * 1024 * 1024


def _rms(x, g):
    return x * lax.rsqrt(jnp.mean(x * x, axis=-1, keepdims=True) + NORM_EPS) * g


def _dot(a, b):
    return jnp.dot(a, b, preferred_element_type=F32)


def _split3(x):
    a1 = x.astype(BF16)
    r1 = x - a1.astype(F32)
    a2 = r1.astype(BF16)
    a3 = (r1 - a2.astype(F32)).astype(BF16)
    return a1, a2, a3


def _tri_cumsum(tri, x):
    a1, a2, a3 = _split3(x)
    return (_dot(tri, a1) + _dot(tri, a2)) + _dot(tri, a3)


def _log_sigmoid(z):
    return jnp.minimum(z, 0.0) - jnp.log1p(jnp.exp(-jnp.abs(z)))


def _rope_tables(pos, inv):
    ang = pos * inv
    return jnp.cos(ang), jnp.sin(ang)


def _rope_chunk(z, cos, sin_signed, first_half):
    swapped = jnp.where(first_half, pltpu.roll(z, LANES - 32, 1), pltpu.roll(z, 32, 1))
    return z * cos + swapped * sin_signed


def _proj_prompt_kernel(x_ref, g_ref, wq_ref, wk_ref, wv_ref, wf_ref, bf_ref, inv_ref, tri_ref, sel_ref,
                        ka_ref, va_ref, kfT_ref, vfT_ref, lfT_ref, katt_ref, qT_ref, vT_ref, kx_ref, yq_ref,
                        carry_ref, cosb_ref, sinb_ref, *, T):
    b = pl.program_id(0)
    s = pl.program_id(1)
    lane = lax.broadcasted_iota(jnp.int32, (T, LANES), 1)
    first_half = (lane % 64) < 32

    @pl.when((b == 0) & (s == 0))
    def _():
        row = lax.broadcasted_iota(jnp.int32, (T, LANES), 0).astype(F32)
        cosb, sinb = _rope_tables(row, inv_ref[...])
        cosb_ref[...] = cosb
        sinb_ref[...] = sinb

    @pl.when(s == 0)
    def _():
        carry_ref[...] = jnp.zeros_like(carry_ref)

    base = jnp.full((8, LANES), s * T, jnp.int32).astype(F32)
    ca, sa = _rope_tables(base, inv_ref[...])
    ca, sa = ca[0:1], sa[0:1]
    cosb, sinb = cosb_ref[...], sinb_ref[...]
    cos = ca * cosb - sa * sinb
    sin = sa * cosb + ca * sinb
    sin_signed = jnp.where(first_half, -sin, sin)

    h = _rms(x_ref[0], g_ref[...]).astype(BF16)

    zq = _dot(h, wq_ref[...])
    for p in range(4):
        q = _rope_chunk(zq[:, p * LANES:(p + 1) * LANES], cos, sin_signed, first_half)
        qT_ref[0, p] = (q * (QK_SCALE * LOG2E)).T.astype(BF16)
    for p in range(4):
        q = zq[:, GROUP_COLS + p * LANES:GROUP_COLS + (p + 1) * LANES]
        qT_ref[0, 4 + p] = (q * (QK_SCALE * LOG2E)).T.astype(BF16)

    zk = _dot(h, wk_ref[...])
    for p in range(4):
        k = _rope_chunk(zk[:, p * LANES:(p + 1) * LANES], cos, sin_signed, first_half)
        ka_ref[0, pl.ds(p, T, stride=DA_HEADS), :] = k
        katt_ref[0, p] = k.astype(BF16)
    for p in range(4):
        kf = zk[:, GROUP_COLS + p * LANES:GROUP_COLS + (p + 1) * LANES]
        kfT_ref[0, p * LANES:(p + 1) * LANES, :] = kf.T
        katt_ref[0, 4 + p] = kf.astype(BF16)

    zv = _dot(h, wv_ref[...])
    for p in range(4):
        v = zv[:, p * LANES:(p + 1) * LANES]
        va_ref[0, pl.ds(p, T, stride=DA_HEADS), :] = v
        vT_ref[0, p] = v.T.astype(BF16)
    for p in range(4):
        vT = zv[:, GROUP_COLS + p * LANES:GROUP_COLS + (p + 1) * LANES].T
        vfT_ref[0, p * LANES:(p + 1) * LANES, :] = vT
        vT_ref[0, 4 + p] = vT.astype(BF16)

    zf = _dot(h, wf_ref[...]) + bf_ref[...]
    lf = jnp.where(lane < FX_HEADS, _log_sigmoid(zf), 0.0)
    lfT_ref[0] = lf.T[:FX_HEADS, :]
    cs = _tri_cumsum(tri_ref[...], lf) + carry_ref[0:1, :]
    carry_ref[0:1, :] = cs[T - 1:T, :]

    cs2 = cs * LOG2E
    k1, k2, k3 = _split3(cs2)
    aux = (_dot(k1, sel_ref[0]) + _dot(k2, sel_ref[1])) + _dot(k3, sel_ref[2])
    lane4 = lax.broadcasted_iota(jnp.int32, aux.shape, 1) % LANES
    aux = jnp.where((lane4 >= AUX_ONE) & (lane4 < AUX_ONE + 3), 1.0, aux)
    for g in range(4):
        kx_ref[0, g] = aux[:, g * LANES:(g + 1) * LANES].astype(BF16)
    q1, q2, q3 = (t.astype(F32) for t in _split3(cs2.T))
    r = lax.broadcasted_iota(jnp.int32, (AUX_ROWS, T), 0)
    for hd in range(FX_HEADS):
        key_rows = (r >= AUX_KEY[hd % 2]) & (r < AUX_KEY[hd % 2] + 3)
        blk = jnp.where(key_rows, -1.0, 0.0)
        for t, qt in enumerate((q1, q2, q3)):
            blk = jnp.where(r == AUX_ONE + t, jnp.broadcast_to(qt[hd:hd + 1], (AUX_ROWS, T)), blk)
        yq_ref[0, hd] = blk.astype(BF16)


def _const_spec(shape):
    n = len(shape)
    return pl.BlockSpec(shape, lambda *_: (0,) * n, pipeline_mode=pl.Buffered(1))


def _proj_prompt(x, g, wq, wk, wv, wf, bf, inv, tri, sel):
    B, S, D = x.shape
    T = PROJ_T
    tok = lambda w: pl.BlockSpec((1, T, w), lambda b, s: (b, s, 0))
    rows4 = pl.BlockSpec((1, T * DA_HEADS, LANES), lambda b, s: (b, s, 0))
    featT = lambda n: pl.BlockSpec((1, n, T), lambda b, s: (b, 0, s))
    out_shape = (
        jax.ShapeDtypeStruct((B, S * DA_HEADS, LANES), F32),
        jax.ShapeDtypeStruct((B, S * DA_HEADS, LANES), F32),
        jax.ShapeDtypeStruct((B, GROUP_COLS, S), F32),
        jax.ShapeDtypeStruct((B, GROUP_COLS, S), F32),
        jax.ShapeDtypeStruct((B, FX_HEADS, S), F32),
        jax.ShapeDtypeStruct((B, N_PAIRS, S, LANES), BF16),
        jax.ShapeDtypeStruct((B, N_PAIRS, LANES, S), BF16),
        jax.ShapeDtypeStruct((B, N_PAIRS, LANES, S), BF16),
        jax.ShapeDtypeStruct((B, 4, S, LANES), BF16),
        jax.ShapeDtypeStruct((B, FX_HEADS, AUX_ROWS, S), BF16),
    )
    out_specs = (
        rows4, rows4, featT(GROUP_COLS), featT(GROUP_COLS), featT(FX_HEADS),
        pl.BlockSpec((1, N_PAIRS, T, LANES), lambda b, s: (b, 0, s, 0)),
        pl.BlockSpec((1, N_PAIRS, LANES, T), lambda b, s: (b, 0, 0, s)),
        pl.BlockSpec((1, N_PAIRS, LANES, T), lambda b, s: (b, 0, 0, s)),
        pl.BlockSpec((1, 4, T, LANES), lambda b, s: (b, 0, s, 0)),
        pl.BlockSpec((1, FX_HEADS, AUX_ROWS, T), lambda b, s: (b, 0, 0, s)),
    )
    in_specs = [tok(D), _const_spec(g.shape), _const_spec(wq.shape), _const_spec(wk.shape),
                _const_spec(wv.shape), _const_spec(wf.shape), _const_spec(bf.shape),
                _const_spec(inv.shape), _const_spec(tri.shape), _const_spec(sel.shape)]
    return pl.pallas_call(
        functools.partial(_proj_prompt_kernel, T=T),
        grid=(B, S // T),
        in_specs=in_specs,
        out_specs=out_specs,
        out_shape=out_shape,
        scratch_shapes=[pltpu.VMEM((8, LANES), F32), pltpu.VMEM((T, LANES), F32), pltpu.VMEM((T, LANES), F32)],
        compiler_params=pltpu.CompilerParams(dimension_semantics=("arbitrary", "arbitrary"),
                                             vmem_limit_bytes=VMEM_LIMIT),
        name="proj_prompt",
    )(x, g, wq, wk, wv, wf, bf, inv, tri, sel)


def _lambda_value(lamv_ref):
    v = lamv_ref[...]
    d1 = jnp.sum(v[0:1] * v[1:2], axis=1, keepdims=True)
    d2 = jnp.sum(v[2:3] * v[3:4], axis=1, keepdims=True)
    return jnp.exp(d1) - jnp.exp(d2) + LAM_INIT


def _flash_step(qi, ki, qT_ref, k_ref, kx_ref, vT_ref, yq_ref, lamv_ref, gcol_ref, o_ref, m_ref, acc_ref,
                s_buf, T):
    @pl.when(ki == 0)
    def _():
        m_ref[...] = jnp.full_like(m_ref, NEG_BIG)
        acc_ref[...] = jnp.zeros_like(acc_ref)

    def step(masked):
        if masked:
            key_i = lax.broadcasted_iota(jnp.int32, (T, T), 0)
            qry_i = lax.broadcasted_iota(jnp.int32, (T, T), 1)
            valid = key_i <= qry_i
        zeros = jnp.zeros((64, T), BF16)
        aux_pad = jnp.zeros((LANES - AUX_ROWS, T), BF16)

        n_slots = s_buf.shape[0]

        def scores(i):
            pair, w = i // 2, i % 2
            qTp = qT_ref[0, pair]
            wq = jnp.concatenate([qTp[:64], zeros] if w == 0 else [zeros, qTp[64:]], axis=0)
            if pair < DA_HEADS:
                s = _dot(k_ref[0, pair], wq)
            else:
                lhs = jnp.concatenate([k_ref[0, pair], kx_ref[0, pair - DA_HEADS]], axis=1)
                s = _dot(lhs, jnp.concatenate([wq, yq_ref[0, i - 2 * DA_HEADS], aux_pad], axis=0))
            if masked:
                s = jnp.where(valid, s, NEG_BIG)
            s_buf[i % n_slots] = s
            return jnp.max(s, axis=0, keepdims=True)

        def online(i, tile_max):
            m_old = m_ref[i]
            m_new = jnp.maximum(m_old, tile_max)
            m_ref[i] = m_new
            return jnp.exp2(m_old - m_new), jnp.exp2(s_buf[i % n_slots] - m_new).astype(BF16)

        ones_blk = jnp.where(lax.broadcasted_iota(jnp.int32, (DEN_ROWS, T), 0) == 0, 1.0, 0.0).astype(BF16)

        n_soft = 2 * N_PAIRS
        tile_max = [scores(j) for j in range(QK_AHEAD)]
        for i in range(n_soft):
            if i + QK_AHEAD < n_soft:
                tile_max.append(scores(i + QK_AHEAD))
            alpha, p = online(i, tile_max.pop(0))
            pair = i // 2
            if pair < DA_HEADS:
                vals = vT_ref[0, pair]
            else:
                vals = vT_ref[0, pair, (i % 2) * FX_DIM:(i % 2 + 1) * FX_DIM, :]
            rows = pl.ds(0, vals.shape[0] + DEN_ROWS)
            acc_ref[i, rows, :] = alpha * acc_ref[i, rows, :] + _dot(jnp.concatenate([vals, ones_blk], axis=0), p)

    @pl.when(ki < qi)
    def _():
        step(False)

    @pl.when(ki == qi)
    def _():
        step(True)

        def normalized(i, n_rows):
            return acc_ref[i, 0:n_rows, :] * (1.0 / acc_ref[i, n_rows:n_rows + 1, :])

        lam = _lambda_value(lamv_ref)
        for h in range(DA_HEADS):
            o = normalized(2 * h, DA_V_DIM) - lam * normalized(2 * h + 1, DA_V_DIM)
            ms = jnp.mean(o * o, axis=0, keepdims=True)
            y = o * lax.rsqrt(ms + NORM_EPS) * gcol_ref[...] * (1.0 - LAM_INIT)
            o_ref[0, :, h * LANES:(h + 1) * LANES] = y.T.astype(o_ref.dtype)
        for g in range(FX_HEADS // 2):
            i0 = 2 * DA_HEADS + 2 * g
            y = jnp.concatenate([normalized(i0, FX_DIM), normalized(i0 + 1, FX_DIM)], axis=0)
            o_ref[0, :, GROUP_COLS + g * LANES:GROUP_COLS + (g + 1) * LANES] = y.T.astype(o_ref.dtype)


def _flash_kernel(qtab_ref, ktab_ref, *refs, T):
    t = pl.program_id(1)
    _flash_step(qtab_ref[t], ktab_ref[t], *refs, T)


def _flash_prompt(qT, katt, kx, vT, yq, lamv, gcol):
    B, _, _, S = qT.shape
    T = ATT_T
    n = S // T
    qtab = jnp.asarray([qi for qi in range(n) for _ in range(qi + 1)], jnp.int32)
    ktab = jnp.asarray([ki for qi in range(n) for ki in range(qi + 1)], jnp.int32)
    q_cols = lambda rows: pl.BlockSpec((1, rows[0], rows[1], T), lambda b, t, qt, kt: (b, 0, 0, qt[t]))
    in_specs = [
        q_cols((N_PAIRS, LANES)),
        pl.BlockSpec((1, N_PAIRS, T, LANES), lambda b, t, qt, kt: (b, 0, kt[t], 0)),
        pl.BlockSpec((1, 4, T, LANES), lambda b, t, qt, kt: (b, 0, kt[t], 0)),
        pl.BlockSpec((1, N_PAIRS, LANES, T), lambda b, t, qt, kt: (b, 0, 0, kt[t])),
        q_cols((FX_HEADS, AUX_ROWS)),
        pl.BlockSpec(lamv.shape, lambda b, t, qt, kt: (0, 0)),
        pl.BlockSpec(gcol.shape, lambda b, t, qt, kt: (0, 0)),
    ]
    return pl.pallas_call(
        functools.partial(_flash_kernel, T=T),
        grid_spec=pltpu.PrefetchScalarGridSpec(
            num_scalar_prefetch=2,
            grid=(B, n * (n + 1) // 2),
            in_specs=in_specs,
            out_specs=pl.BlockSpec((1, T, 2 * GROUP_COLS), lambda b, t, qt, kt: (b, qt[t], 0)),
            scratch_shapes=[
                pltpu.VMEM((2 * N_PAIRS, 1, T), F32),
                pltpu.VMEM((2 * N_PAIRS, DA_V_DIM + DEN_ROWS, T), F32),
                pltpu.VMEM((QK_AHEAD + 1, T, T), F32),
            ],
        ),
        out_shape=jax.ShapeDtypeStruct((B, S, 2 * GROUP_COLS), BF16),
        compiler_params=pltpu.CompilerParams(dimension_semantics=("arbitrary", "arbitrary"),
                                             vmem_limit_bytes=VMEM_LIMIT),
        name="flash_prompt",
    )(qtab, ktab, qT, katt, kx, vT, yq, lamv, gcol)


def _gated_chunk(h, g, gm1, gm2, wu_ref, wd_ref, cw_ref, cb_ref, c0):
    cols = pl.ds(c0, FF_CHUNK)
    u = _dot(h, wu_ref[:, cols])
    cv = cb_ref[:, cols] + cw_ref[0:1, cols] * gm2 + cw_ref[1:2, cols] * gm1 + cw_ref[2:3, cols] * g
    act = jax.nn.gelu(cv, approximate=True) * u
    return _dot(act.astype(BF16), wd_ref[cols, :])


def _post_prompt_kernel(x_ref, o_ref, wo_ref, g1_ref, g2_ref, wg_ref, wu_ref, cw_ref, cb_ref, wd_ref, g3_ref,
                        y_ref, st_ref, gext_ref, *, T):
    s = pl.program_id(1)
    n_ff = wg_ref.shape[1]

    @pl.when(s == 0)
    def _():
        gext_ref[0:8, :] = jnp.zeros((8, n_ff), F32)

    x1 = x_ref[0] + _rms(_dot(o_ref[0], wo_ref[...]), g1_ref[...])
    h = _rms(x1, g2_ref[...]).astype(BF16)
    out = jnp.zeros_like(x1)
    for c0 in range(0, n_ff, FF_CHUNK):
        cols = pl.ds(c0, FF_CHUNK)
        g = _dot(h, wg_ref[:, cols])
        gext_ref[8:T + 8, cols] = g
        gm1 = gext_ref[7:T + 7, cols]
        gm2 = gext_ref[6:T + 6, cols]
        out = out + _gated_chunk(h, g, gm1, gm2, wu_ref, wd_ref, cw_ref, cb_ref, c0)
    last = gext_ref[T + 6:T + 8, :]
    gext_ref[6:8, :] = last
    y_ref[0] = x1 + _rms(out, g3_ref[...])

    @pl.when(s == pl.num_programs(1) - 1)
    def _():
        st_ref[0] = last


def _post_prompt(x, o, wo, g1, g2, wg, wu, cw, cb, wd, g3):
    B, S, D = x.shape
    T = FFN_T
    n_ff = wg.shape[1]
    tok = lambda w: pl.BlockSpec((1, T, w), lambda b, s: (b, s, 0))
    in_specs = [tok(D), tok(D)] + [_const_spec(a.shape) for a in (wo, g1, g2, wg, wu, cw, cb, wd, g3)]
    return pl.pallas_call(
        functools.partial(_post_prompt_kernel, T=T),
        grid=(B, S // T),
        in_specs=in_specs,
        out_specs=(tok(D), pl.BlockSpec((1, CONV_W - 1, n_ff), lambda b, s: (b, 0, 0))),
        out_shape=(jax.ShapeDtypeStruct((B, S, D), F32), jax.ShapeDtypeStruct((B, CONV_W - 1, n_ff), F32)),
        scratch_shapes=[pltpu.VMEM((T + 8, n_ff), F32)],
        compiler_params=pltpu.CompilerParams(dimension_semantics=("arbitrary", "arbitrary"),
                                             vmem_limit_bytes=VMEM_LIMIT),
        name="post_prompt",
    )(x, o, wo, g1, g2, wg, wu, cw, cb, wd, g3)


def _proj_sample_kernel(x_ref, g_ref, wq_ref, wk_ref, wv_ref, wf_ref, bf_ref, inv_ref, pos_ref,
                        qa_ref, qf_ref, ka_ref, va_ref, kf_ref, vf_ref, lf_ref):
    n = x_ref.shape[0]
    lane = lax.broadcasted_iota(jnp.int32, (n, LANES), 1)
    first_half = (lane % 64) < 32
    cos, sin = _rope_tables(jnp.broadcast_to(pos_ref[...], (n, LANES)), inv_ref[...])
    sin_signed = jnp.where(first_half, -sin, sin)
    h = _rms(x_ref[...], g_ref[...]).astype(BF16)

    zq = _dot(h, wq_ref[...])
    zk = _dot(h, wk_ref[...])
    for p in range(4):
        cols = slice(p * LANES, (p + 1) * LANES)
        qa_ref[:, cols] = _rope_chunk(zq[:, cols], cos, sin_signed, first_half) * QK_SCALE
        ka_ref[:, cols] = _rope_chunk(zk[:, cols], cos, sin_signed, first_half)
    qf_ref[...] = zq[:, GROUP_COLS:] * QK_SCALE
    kf_ref[...] = zk[:, GROUP_COLS:]
    zv = _dot(h, wv_ref[...])
    va_ref[...] = zv[:, :GROUP_COLS]
    vf_ref[...] = zv[:, GROUP_COLS:]
    lf = _log_sigmoid(_dot(h, wf_ref[...]) + bf_ref[...])
    lf_ref[...] = jnp.where(lane < FX_HEADS, lf, 0.0)


def _proj_sample(x, g, wq, wk, wv, wf, bf, inv, pos):
    n = x.shape[0]
    row = lambda w: jax.ShapeDtypeStruct((n, w), F32)
    return pl.pallas_call(
        _proj_sample_kernel,
        out_shape=(row(GROUP_COLS),) * 6 + (row(LANES),),
        compiler_params=pltpu.CompilerParams(vmem_limit_bytes=VMEM_LIMIT),
        name="proj_sample",
    )(x, g, wq, wk, wv, wf, bf, inv, pos)


def _forget_bias_kernel(pt_ref, *refs, n_pages):
    del pt_ref
    page_refs = refs[:n_pages]
    lfnew_ref, upper_ref, later_ref, out_ref = refs[n_pages:]
    x = jnp.concatenate([r[0] for r in page_refs], axis=0)
    a1, a2, a3 = _split3(x)
    within = (_dot(a1, upper_ref[...]) + _dot(a2, upper_ref[...])) + _dot(a3, upper_ref[...])
    page_total = jnp.broadcast_to(within[:, PAGE_SIZE - 1:PAGE_SIZE], x.shape)
    t1, t2, t3 = _split3(page_total)
    suffix = (_dot(later_ref[...], t1) + _dot(later_ref[...], t2)) + _dot(later_ref[...], t3)
    lf_new = jnp.concatenate([lfnew_ref[0]] * n_pages, axis=0)
    out_ref[0] = (suffix + lf_new) - within


def _forget_bias(page_table, cache_lf, lf_new, upper, later):
    n, n_pages = page_table.shape
    rows = n_pages * FX_HEADS

    def page_spec(i):
        return pl.BlockSpec((1, FX_HEADS, PAGE_SIZE), lambda b, pt: (pt[b, i], 0, 0))

    in_specs = [page_spec(i) for i in range(n_pages)] + [
        pl.BlockSpec((1, FX_HEADS, LANES), lambda b, pt: (b, 0, 0)),
        pl.BlockSpec(upper.shape, lambda b, pt: (0, 0)),
        pl.BlockSpec(later.shape, lambda b, pt: (0, 0)),
    ]
    return pl.pallas_call(
        functools.partial(_forget_bias_kernel, n_pages=n_pages),
        grid_spec=pltpu.PrefetchScalarGridSpec(
            num_scalar_prefetch=1,
            grid=(n,),
            in_specs=in_specs,
            out_specs=pl.BlockSpec((1, rows, PAGE_SIZE), lambda b, pt: (b, 0, 0)),
        ),
        out_shape=jax.ShapeDtypeStruct((n, rows, PAGE_SIZE), F32),
        compiler_params=pltpu.CompilerParams(dimension_semantics=("arbitrary",), vmem_limit_bytes=VMEM_LIMIT),
        name="forget_bias",
    )(page_table, *([cache_lf] * n_pages), lf_new, upper, later)


def _decode_step(c, n_chunks, refs, oa_ref, of_ref, m_ref, l_ref, accd_ref, accf_ref, pps):
    kd_refs, vd_refs = refs[0:pps], refs[pps:2 * pps]
    kf_refs, vf_refs = refs[2 * pps:3 * pps], refs[3 * pps:4 * pps]
    bias_ref, qa_ref, qf_ref, kan_ref, van_ref, kfn_ref, vfn_ref, lamv_ref = refs[4 * pps:]
    nt = (((1,), (1,)), ((), ()))

    lane4 = lax.broadcasted_iota(jnp.int32, (DA_HEADS, LANES), 1)
    q4 = qa_ref[0]
    qd = jnp.concatenate([jnp.where(lane4 < DA_QK_DIM, q4, 0.0), jnp.where(lane4 >= DA_QK_DIM, q4, 0.0)], axis=0)
    row = lax.broadcasted_iota(jnp.int32, (FX_HEADS, GROUP_COLS), 0)
    col = lax.broadcasted_iota(jnp.int32, (FX_HEADS, GROUP_COLS), 1)
    own_f = col // FX_DIM == row
    qf = jnp.where(own_f, jnp.broadcast_to(qf_ref[0], (FX_HEADS, GROUP_COLS)), 0.0)
    own_d = col % DA_HEADS == row % DA_HEADS

    @pl.when(c == 0)
    def _():
        m_ref[...] = jnp.full_like(m_ref, NEG_BIG)
        l_ref[...] = jnp.zeros_like(l_ref)
        accd_ref[...] = jnp.zeros_like(accd_ref)
        accf_ref[...] = jnp.zeros_like(accf_ref)

    def online(grp, s):
        m_old = m_ref[grp][:, 0:1]
        m_new = jnp.maximum(m_old, jnp.max(s, axis=1, keepdims=True))
        alpha = jnp.exp(m_old - m_new)
        p = jnp.exp(s - m_new)
        l_new = alpha * l_ref[grp][:, 0:1] + jnp.sum(p, axis=1, keepdims=True)
        m_ref[grp] = jnp.broadcast_to(m_new, (8, LANES))
        l_ref[grp] = jnp.broadcast_to(l_new, (8, LANES))
        return alpha, p.astype(BF16)

    qdb = qd.astype(BF16)
    s = jnp.concatenate(
        [jnp.where(own_d, lax.dot_general(qdb, kd_refs[i][0].astype(BF16), nt, preferred_element_type=F32), NEG_BIG)
         for i in range(pps)], axis=1)
    alpha, p = online(0, s)
    w = GROUP_COLS
    pv = _dot(p[:, 0:w], vd_refs[0][0].astype(BF16))
    for i in range(1, pps):
        pv = pv + _dot(p[:, i * w:(i + 1) * w], vd_refs[i][0].astype(BF16))
    accd_ref[...] = alpha * accd_ref[...] + pv

    qfb = qf.astype(BF16)
    s = jnp.concatenate([_dot(qfb, kf_refs[i][0].astype(BF16)) + bias_ref[0, i] for i in range(pps)], axis=1)
    alpha, p = online(1, s)
    w = PAGE_SIZE
    pv = lax.dot_general(p[:, 0:w], vf_refs[0][0].astype(BF16), nt, preferred_element_type=F32)
    for i in range(1, pps):
        pv = pv + lax.dot_general(p[:, i * w:(i + 1) * w], vf_refs[i][0].astype(BF16), nt,
                                  preferred_element_type=F32)
    accf_ref[...] = alpha * accf_ref[...] + pv

    @pl.when(c == n_chunks - 1)
    def _():
        def finish(grp, q, k_new, v_new, acc):
            s = jnp.sum(q * k_new, axis=1, keepdims=True)
            m_old = m_ref[grp][:, 0:1]
            m_new = jnp.maximum(m_old, s)
            alpha = jnp.exp(m_old - m_new)
            p = jnp.exp(s - m_new)
            l_new = alpha * l_ref[grp][:, 0:1] + p
            return (alpha * acc + p * v_new) * (1.0 / l_new)

        twice = lambda a: jnp.concatenate([a, a], axis=0)
        od = finish(0, qd, twice(kan_ref[0]), twice(van_ref[0]), accd_ref[...])
        oa_ref[0] = od[0:DA_HEADS] - _lambda_value(lamv_ref) * od[DA_HEADS:]
        of = finish(1, qf, kfn_ref[0], vfn_ref[0], accf_ref[...])
        of_ref[0] = jnp.sum(jnp.where(own_f, of, 0.0), axis=0, keepdims=True)


def _decode_kernel(pt_ref, *refs, pps):
    del pt_ref
    n_in = 4 * pps + 8
    _decode_step(pl.program_id(1), pl.num_programs(1), refs[:n_in], *refs[n_in:], pps)


def _decode_attention(page_table, cdk, cdv, cfk, cfv, bias, qa, qf, ka_new, va_new, kf_new, vf_new, lamv):
    n, n_pages = page_table.shape
    pps = PAGES_PER_STEP
    steps = n_pages // pps

    def page_spec(i):
        return pl.BlockSpec((1, GROUP_COLS, PAGE_SIZE), lambda b, c, pt: (pt[b, c * pps + i], 0, 0))

    row_spec = pl.BlockSpec((1, 1, GROUP_COLS), lambda b, c, pt: (b, 0, 0))
    head_spec = pl.BlockSpec((1, DA_HEADS, LANES), lambda b, c, pt: (b, 0, 0))
    in_specs = ([page_spec(i) for i in range(pps)] * 4
                + [pl.BlockSpec((1, pps, FX_HEADS, PAGE_SIZE), lambda b, c, pt: (b, c, 0, 0))]
                + [head_spec, row_spec, head_spec, head_spec, row_spec, row_spec]
                + [pl.BlockSpec(lamv.shape, lambda b, c, pt: (0, 0))])
    caches = [cdk] * pps + [cdv] * pps + [cfk] * pps + [cfv] * pps
    return pl.pallas_call(
        functools.partial(_decode_kernel, pps=pps),
        grid_spec=pltpu.PrefetchScalarGridSpec(
            num_scalar_prefetch=1,
            grid=(n, steps),
            in_specs=in_specs,
            out_specs=(head_spec, row_spec),
            scratch_shapes=[pltpu.VMEM((2, 8, LANES), F32), pltpu.VMEM((2, 8, LANES), F32),
                            pltpu.VMEM((8, LANES), F32), pltpu.VMEM((8, GROUP_COLS), F32)],
        ),
        out_shape=(jax.ShapeDtypeStruct((n, DA_HEADS, LANES), F32), jax.ShapeDtypeStruct((n, 1, GROUP_COLS), F32)),
        compiler_params=pltpu.CompilerParams(dimension_semantics=("arbitrary", "arbitrary"),
                                             vmem_limit_bytes=VMEM_LIMIT),
        name="decode_attention",
    )(page_table, *caches, bias, qa, qf, ka_new, va_new, kf_new, vf_new, lamv)


def _post_sample_kernel(x_ref, oa_ref, of_ref, sg_ref, wo_ref, g1_ref, g2_ref, wg_ref, wu_ref, cw_ref, cb_ref,
                        wd_ref, g3_ref, p0_ref, p1_ref, y_ref, gate_ref):
    n_ff = wg_ref.shape[1]
    heads = [_rms(oa_ref[:, h * LANES:(h + 1) * LANES], sg_ref[...]) * (1.0 - LAM_INIT) for h in range(DA_HEADS)]
    o = jnp.concatenate(heads + [of_ref[...]], axis=1).astype(BF16)
    x1 = x_ref[...] + _rms(_dot(o, wo_ref[...]), g1_ref[...])
    h = _rms(x1, g2_ref[...]).astype(BF16)
    out = jnp.zeros_like(x1)
    for c0 in range(0, n_ff, FF_CHUNK):
        cols = pl.ds(c0, FF_CHUNK)
        g = _dot(h, wg_ref[:, cols])
        gate_ref[:, cols] = g
        out = out + _gated_chunk(h, g, p1_ref[:, cols], p0_ref[:, cols], wu_ref, wd_ref, cw_ref, cb_ref, c0)
    y_ref[...] = x1 + _rms(out, g3_ref[...])


def _post_sample(x, oa, of, sg, wo, g1, g2, wg, wu, cw, cb, wd, g3, p0, p1):
    n, d = x.shape
    return pl.pallas_call(
        _post_sample_kernel,
        out_shape=(jax.ShapeDtypeStruct((n, d), F32), jax.ShapeDtypeStruct((n, wg.shape[1]), F32)),
        compiler_params=pltpu.CompilerParams(vmem_limit_bytes=VMEM_LIMIT),
        name="post_sample",
    )(x, oa, of, sg, wo, g1, g2, wg, wu, cw, cb, wd, g3, p0, p1)


def kernel(x_prompt, x_sample, cache_diff_k, cache_diff_v, cache_fox_k, cache_fox_v, cache_fox_logf, state_ffn_conv, page_table, attn_pre_g, w_in, b_f, lam_q1, lam_k1, lam_q2, lam_k2, subln_g, w_o, attn_post_g, ffn_pre_g, w_gate, w_up, conv_w, conv_b, w_down, ffn_post_g):
    B, S, D = x_prompt.shape
    DB, DS, _ = x_sample.shape
    assert DS == 1 and w_in.shape[0] == 1, "one layer, one new token per sample"
    n_pool = cache_diff_k.shape[1]
    n_pages = page_table.shape[1]
    past_len = n_pages * PAGE_SIZE
    l = 0

    w = w_in[l]
    sec = lambda i: w[:, i * GROUP_COLS:(i + 1) * GROUP_COLS]
    wq = jnp.concatenate([sec(0), sec(3)], axis=1).astype(BF16)
    wk = jnp.concatenate([sec(1), sec(4)], axis=1).astype(BF16)
    wv = jnp.concatenate([sec(2), sec(5)], axis=1).astype(BF16)
    wf = jnp.pad(w[:, 6 * GROUP_COLS:], ((0, 0), (0, LANES - FX_HEADS))).astype(BF16)
    bf = jnp.pad(b_f[l], (0, LANES - FX_HEADS)).reshape(1, LANES)
    wo, wg, wu, wd = (a[l].astype(BF16) for a in (w_o, w_gate, w_up, w_down))
    row = lambda a: a[l].reshape(1, -1)
    g_pre, g_post, g_ffn, g_out, sg, cb = (row(a) for a in (attn_pre_g, attn_post_g, ffn_pre_g, ffn_post_g, subln_g, conv_b))
    cw = conv_w[l]
    half = DA_QK_DIM // 2
    inv = ROPE_THETA ** (-jnp.arange(half, dtype=F32) / half)
    inv = jnp.tile(inv, LANES // half).reshape(1, LANES)
    tri = jnp.tril(jnp.ones((CUM_T, CUM_T), BF16))
    lamv = jnp.stack([lam_q1[l], lam_k1[l], lam_q2[l], lam_k2[l]])

    sel = np.zeros((3, LANES, 4 * LANES), np.float32)
    for hd in range(FX_HEADS):
        for term in range(3):
            sel[term, hd, (hd // 2) * LANES + AUX_KEY[hd % 2] + term] = 1.0
    ka4, va4, kfT, vfT, lfT, katt, qT, vT, kx, yq = _proj_prompt(x_prompt, g_pre, wq, wk, wv, wf, bf, inv, tri,
                                                                 jnp.asarray(sel, BF16))
    heads_last = lambda t: jnp.transpose(t.reshape(B, FX_HEADS, FX_DIM, S), (0, 3, 1, 2))
    kf, vf, logf = heads_last(kfT), heads_last(vfT), jnp.transpose(lfT, (0, 2, 1))

    pos = jnp.full((1, LANES), past_len, F32)
    qa_s, qf_s, ka_s, va_s, kf_s, vf_s, lf_s = _proj_sample(x_sample[:, 0, :], g_pre, wq, wk, wv, wf, bf, inv, pos)
    rows_d = lambda c: c[l].reshape(n_pool, PAGE_SIZE * DA_HEADS, 2 * DA_QK_DIM)
    rows_f = lambda c: jnp.transpose(c[l], (0, 2, 3, 1)).reshape(n_pool, GROUP_COLS, PAGE_SIZE)
    lf_pages = jnp.transpose(cache_fox_logf[l], (0, 2, 1))
    key_i = jnp.arange(PAGE_SIZE)
    upper = (key_i[:, None] <= key_i[None, :]).astype(BF16)
    row_i = jnp.arange(n_pages * FX_HEADS)
    later = ((row_i[:, None] % FX_HEADS == row_i[None, :] % FX_HEADS)
             & (row_i[None, :] // FX_HEADS >= row_i[:, None] // FX_HEADS)).astype(BF16)
    lf_new = jnp.broadcast_to(lf_s[:, :FX_HEADS, None], (DB, FX_HEADS, LANES))
    bias = _forget_bias(page_table, lf_pages, lf_new, upper, later).reshape(DB, n_pages, FX_HEADS, PAGE_SIZE)
    r3 = lambda a: a.reshape(DB, 1, GROUP_COLS)
    r4 = lambda a: a.reshape(DB, DA_HEADS, LANES)
    oa_s, of_s = _decode_attention(page_table, rows_d(cache_diff_k), rows_d(cache_diff_v), rows_f(cache_fox_k),
                                   rows_f(cache_fox_v), bias, r4(qa_s), r3(qf_s), r4(ka_s), r4(va_s), r3(kf_s),
                                   r3(vf_s), lamv)
    o = _flash_prompt(qT, katt, kx, vT, yq, lamv, sg.reshape(DA_V_DIM, 1))
    y_prompt, conv_p = _post_prompt(x_prompt, o, wo, g_post, g_ffn, wg, wu, cw, cb, wd, g_out)
    prev = state_ffn_conv[l]
    y_sample, gate_s = _post_sample(x_sample[:, 0, :], oa_s.reshape(DB, GROUP_COLS), of_s[:, 0, :], sg, wo, g_post,
                                    g_ffn, wg, wu, cw, cb, wd, g_out, prev[:, 0, :], prev[:, 1, :])
    conv_s = jnp.stack([prev[:, 1, :], gate_s], axis=1)

    lead = lambda a, shape: a.reshape((1,) + shape)
    return (
        y_prompt,
        y_sample.reshape(DB, DS, D),
        lead(ka4, (B, S, DA_HEADS, 2 * DA_QK_DIM)),
        lead(va4, (B, S, DA_HEADS, DA_V_DIM)),
        lead(kf, (B, S, FX_HEADS, FX_DIM)),
        lead(vf, (B, S, FX_HEADS, FX_DIM)),
        lead(logf, (B, S, FX_HEADS)),
        lead(conv_p, (B, CONV_W - 1, conv_p.shape[-1])),
        lead(ka_s, (DB, DS, DA_HEADS, 2 * DA_QK_DIM)),
        lead(va_s, (DB, DS, DA_HEADS, DA_V_DIM)),
        lead(kf_s, (DB, DS, FX_HEADS, FX_DIM)),
        lead(vf_s, (DB, DS, FX_HEADS, FX_DIM)),
        lead(lf_s[:, :FX_HEADS], (DB, DS, FX_HEADS)),
        lead(conv_s, (DB, CONV_W - 1, conv_s.shape[-1])),
    )
```

```python
import functools
import math

import jax
import jax.numpy as jnp
import numpy as np
from jax import lax
from jax.experimental import pallas as pl
from jax.experimental.pallas import tpu as pltpu

F32 = jnp.float32
BF16 = jnp.bfloat16

DA_HEADS = 4
DA_QK_DIM = 64
DA_V_DIM = 128
FX_HEADS = 8
FX_DIM = 64
GROUP_COLS = 512
CONV_W = 3
ROPE_THETA = 10000.0
NORM_EPS = 1e-6
PAGE_SIZE = 128
LAM_INIT = 0.8 - 0.6 * math.exp(-0.3 * 0)
QK_SCALE = 0.125

LANES = 128
N_PAIRS = 8
NEG_BIG = -1e30
LOG2E = 1.4426950408889634
AUX_ROWS = 16
AUX_KEY = (0, 3)
AUX_ONE = 6
DEN_ROWS = 16

PROJ_T = 512
ATT_T = 512
FFN_T = 512
MXU_TILE = 256
QK_AHEAD = 2
CUM_T = 512
PAGES_PER_STEP = 16
VMEM_LIMIT = 56 * 1024 * 1024


def _rms(x, g):
    return x * lax.rsqrt(jnp.mean(x * x, axis=-1, keepdims=True) + NORM_EPS) * g


def _dot(a, b):
    return jnp.dot(a, b, preferred_element_type=F32)


def _split3(x):
    a1 = x.astype(BF16)
    r1 = x - a1.astype(F32)
    a2 = r1.astype(BF16)
    a3 = (r1 - a2.astype(F32)).astype(BF16)
    return a1, a2, a3


def _tri_cumsum(tri, x):
    a1, a2, a3 = _split3(x)
    return (_dot(tri, a1) + _dot(tri, a2)) + _dot(tri, a3)


def _log_sigmoid(z):
    return jnp.minimum(z, 0.0) - jnp.log1p(jnp.exp(-jnp.abs(z)))


def _rope_tables(pos, inv):
    ang = pos * inv
    return jnp.cos(ang), jnp.sin(ang)


def _rope_chunk(z, cos, sin_signed, first_half):
    swapped = jnp.where(first_half, pltpu.roll(z, LANES - 32, 1), pltpu.roll(z, 32, 1))
    return z * cos + swapped * sin_signed


def _proj_prompt_kernel(x_ref, g_ref, wq_ref, wk_ref, wv_ref, wf_ref, bf_ref, inv_ref, tri_ref, sel_ref,
                        ka_ref, va_ref, kfT_ref, vfT_ref, lfT_ref, katt_ref, qT_ref, vT_ref, kx_ref, yq_ref,
                        carry_ref, cosb_ref, sinb_ref, *, T):
    b = pl.program_id(0)
    s = pl.program_id(1)
    lane = lax.broadcasted_iota(jnp.int32, (T, LANES), 1)
    first_half = (lane % 64) < 32

    @pl.when((b == 0) & (s == 0))
    def _():
        row = lax.broadcasted_iota(jnp.int32, (T, LANES), 0).astype(F32)
        cosb, sinb = _rope_tables(row, inv_ref[...])
        cosb_ref[...] = cosb
        sinb_ref[...] = sinb

    @pl.when(s == 0)
    def _():
        carry_ref[...] = jnp.zeros_like(carry_ref)

    base = jnp.full((8, LANES), s * T, jnp.int32).astype(F32)
    ca, sa = _rope_tables(base, inv_ref[...])
    ca, sa = ca[0:1], sa[0:1]
    cosb, sinb = cosb_ref[...], sinb_ref[...]
    cos = ca * cosb - sa * sinb
    sin = sa * cosb + ca * sinb
    sin_signed = jnp.where(first_half, -sin, sin)

    h = _rms(x_ref[0], g_ref[...]).astype(BF16)

    zq = _dot(h, wq_ref[...])
    for p in range(4):
        q = _rope_chunk(zq[:, p * LANES:(p + 1) * LANES], cos, sin_signed, first_half)
        qT_ref[0, p] = (q * (QK_SCALE * LOG2E)).T.astype(BF16)
    for p in range(4):
        q = zq[:, GROUP_COLS + p * LANES:GROUP_COLS + (p + 1) * LANES]
        qT_ref[0, 4 + p] = (q * (QK_SCALE * LOG2E)).T.astype(BF16)

    zk = _dot(h, wk_ref[...])
    for p in range(4):
        k = _rope_chunk(zk[:, p * LANES:(p + 1) * LANES], cos, sin_signed, first_half)
        ka_ref[0, pl.ds(p, T, stride=DA_HEADS), :] = k
        katt_ref[0, p] = k.astype(BF16)
    for p in range(4):
        kf = zk[:, GROUP_COLS + p * LANES:GROUP_COLS + (p + 1) * LANES]
        kfT_ref[0, p * LANES:(p + 1) * LANES, :] = kf.T
        katt_ref[0, 4 + p] = kf.astype(BF16)

    zv = _dot(h, wv_ref[...])
    for p in range(4):
        v = zv[:, p * LANES:(p + 1) * LANES]
        va_ref[0, pl.ds(p, T, stride=DA_HEADS), :] = v
        vT_ref[0, p] = v.T.astype(BF16)
    for p in range(4):
        vT = zv[:, GROUP_COLS + p * LANES:GROUP_COLS + (p + 1) * LANES].T
        vfT_ref[0, p * LANES:(p + 1) * LANES, :] = vT
        vT_ref[0, 4 + p] = vT.astype(BF16)

    zf = _dot(h, wf_ref[...]) + bf_ref[...]
    lf = jnp.where(lane < FX_HEADS, _log_sigmoid(zf), 0.0)
    lfT_ref[0] = lf.T[:FX_HEADS, :]
    cs = _tri_cumsum(tri_ref[...], lf) + carry_ref[0:1, :]
    carry_ref[0:1, :] = cs[T - 1:T, :]

    cs2 = cs * LOG2E
    k1, k2, k3 = _split3(cs2)
    aux = (_dot(k1, sel_ref[0]) + _dot(k2, sel_ref[1])) + _dot(k3, sel_ref[2])
    lane4 = lax.broadcasted_iota(jnp.int32, aux.shape, 1) % LANES
    aux = jnp.where((lane4 >= AUX_ONE) & (lane4 < AUX_ONE + 3), 1.0, aux)
    for g in range(4):
        kx_ref[0, g] = aux[:, g * LANES:(g + 1) * LANES].astype(BF16)
    q1, q2, q3 = (t.astype(F32) for t in _split3(cs2.T))
    r = lax.broadcasted_iota(jnp.int32, (AUX_ROWS, T), 0)
    for hd in range(FX_HEADS):
        key_rows = (r >= AUX_KEY[hd % 2]) & (r < AUX_KEY[hd % 2] + 3)
        blk = jnp.where(key_rows, -1.0, 0.0)
        for t, qt in enumerate((q1, q2, q3)):
            blk = jnp.where(r == AUX_ONE + t, jnp.broadcast_to(qt[hd:hd + 1], (AUX_ROWS, T)), blk)
        yq_ref[0, hd] = blk.astype(BF16)


def _const_spec(shape):
    n = len(shape)
    return pl.BlockSpec(shape, lambda *_: (0,) * n, pipeline_mode=pl.Buffered(1))


def _proj_prompt(x, g, wq, wk, wv, wf, bf, inv, tri, sel):
    B, S, D = x.shape
    T = PROJ_T
    tok = lambda w: pl.BlockSpec((1, T, w), lambda b, s: (b, s, 0))
    rows4 = pl.BlockSpec((1, T * DA_HEADS, LANES), lambda b, s: (b, s, 0))
    featT = lambda n: pl.BlockSpec((1, n, T), lambda b, s: (b, 0, s))
    out_shape = (
        jax.ShapeDtypeStruct((B, S * DA_HEADS, LANES), F32),
        jax.ShapeDtypeStruct((B, S * DA_HEADS, LANES), F32),
        jax.ShapeDtypeStruct((B, GROUP_COLS, S), F32),
        jax.ShapeDtypeStruct((B, GROUP_COLS, S), F32),
        jax.ShapeDtypeStruct((B, FX_HEADS, S), F32),
        jax.ShapeDtypeStruct((B, N_PAIRS, S, LANES), BF16),
        jax.ShapeDtypeStruct((B, N_PAIRS, LANES, S), BF16),
        jax.ShapeDtypeStruct((B, N_PAIRS, LANES, S), BF16),
        jax.ShapeDtypeStruct((B, 4, S, LANES), BF16),
        jax.ShapeDtypeStruct((B, FX_HEADS, AUX_ROWS, S), BF16),
    )
    out_specs = (
        rows4, rows4, featT(GROUP_COLS), featT(GROUP_COLS), featT(FX_HEADS),
        pl.BlockSpec((1, N_PAIRS, T, LANES), lambda b, s: (b, 0, s, 0)),
        pl.BlockSpec((1, N_PAIRS, LANES, T), lambda b, s: (b, 0, 0, s)),
        pl.BlockSpec((1, N_PAIRS, LANES, T), lambda b, s: (b, 0, 0, s)),
        pl.BlockSpec((1, 4, T, LANES), lambda b, s: (b, 0, s, 0)),
        pl.BlockSpec((1, FX_HEADS, AUX_ROWS, T), lambda b, s: (b, 0, 0, s)),
    )
    in_specs = [tok(D), _const_spec(g.shape), _const_spec(wq.shape), _const_spec(wk.shape),
                _const_spec(wv.shape), _const_spec(wf.shape), _const_spec(bf.shape),
                _const_spec(inv.shape), _const_spec(tri.shape), _const_spec(sel.shape)]
    return pl.pallas_call(
        functools.partial(_proj_prompt_kernel, T=T),
        grid=(B, S // T),
        in_specs=in_specs,
        out_specs=out_specs,
        out_shape=out_shape,
        scratch_shapes=[pltpu.VMEM((8, LANES), F32), pltpu.VMEM((T, LANES), F32), pltpu.VMEM((T, LANES), F32)],
        compiler_params=pltpu.CompilerParams(dimension_semantics=("arbitrary", "arbitrary"),
                                             vmem_limit_bytes=VMEM_LIMIT),
        name="proj_prompt",
    )(x, g, wq, wk, wv, wf, bf, inv, tri, sel)


def _lambda_value(lamv_ref):
    v = lamv_ref[...]
    d1 = jnp.sum(v[0:1] * v[1:2], axis=1, keepdims=True)
    d2 = jnp.sum(v[2:3] * v[3:4], axis=1, keepdims=True)
    return jnp.exp(d1) - jnp.exp(d2) + LAM_INIT


def _flash_step(qi, ki, qT_ref, k_ref, kx_ref, vT_ref, yq_ref, lamv_ref, gcol_ref, o_ref, m_ref, acc_ref,
                s_buf, T):
    @pl.when(ki == 0)
    def _():
        m_ref[...] = jnp.full_like(m_ref, NEG_BIG)
        acc_ref[...] = jnp.zeros_like(acc_ref)

    def step(masked):
        if masked:
            key_i = lax.broadcasted_iota(jnp.int32, (T, T), 0)
            qry_i = lax.broadcasted_iota(jnp.int32, (T, T), 1)
            valid = key_i <= qry_i
        zeros = jnp.zeros((64, T), BF16)
        aux_pad = jnp.zeros((LANES - AUX_ROWS, T), BF16)

        n_slots = s_buf.shape[0]

        def scores(i):
            pair, w = i // 2, i % 2
            qTp = qT_ref[0, pair]
            wq = jnp.concatenate([qTp[:64], zeros] if w == 0 else [zeros, qTp[64:]], axis=0)
            if pair < DA_HEADS:
                s = _dot(k_ref[0, pair], wq)
            else:
                lhs = jnp.concatenate([k_ref[0, pair], kx_ref[0, pair - DA_HEADS]], axis=1)
                s = _dot(lhs, jnp.concatenate([wq, yq_ref[0, i - 2 * DA_HEADS], aux_pad], axis=0))
            if masked:
                s = jnp.where(valid, s, NEG_BIG)
            s_buf[i % n_slots] = s
            return jnp.max(s, axis=0, keepdims=True)

        def online(i, tile_max):
            m_old = m_ref[i]
            m_new = jnp.maximum(m_old, tile_max)
            m_ref[i] = m_new
            return jnp.exp2(m_old - m_new), jnp.exp2(s_buf[i % n_slots] - m_new).astype(BF16)

        ones_blk = jnp.where(lax.broadcasted_iota(jnp.int32, (DEN_ROWS, T), 0) == 0, 1.0, 0.0).astype(BF16)

        n_soft = 2 * N_PAIRS
        tile_max = [scores(j) for j in range(QK_AHEAD)]
        for i in range(n_soft):
            if i + QK_AHEAD < n_soft:
                tile_max.append(scores(i + QK_AHEAD))
            alpha, p = online(i, tile_max.pop(0))
            pair = i // 2
            if pair < DA_HEADS:
                vals = vT_ref[0, pair]
            else:
                vals = vT_ref[0, pair, (i % 2) * FX_DIM:(i % 2 + 1) * FX_DIM, :]
            rows = pl.ds(0, vals.shape[0] + DEN_ROWS)
            acc_ref[i, rows, :] = alpha * acc_ref[i, rows, :] + _dot(jnp.concatenate([vals, ones_blk], axis=0), p)

    @pl.when(ki < qi)
    def _():
        step(False)

    @pl.when(ki == qi)
    def _():
        step(True)

        def normalized(i, n_rows):
            return acc_ref[i, 0:n_rows, :] * (1.0 / acc_ref[i, n_rows:n_rows + 1, :])

        lam = _lambda_value(lamv_ref)
        for h in range(DA_HEADS):
            o = normalized(2 * h, DA_V_DIM) - lam * normalized(2 * h + 1, DA_V_DIM)
            ms = jnp.mean(o * o, axis=0, keepdims=True)
            y = o * lax.rsqrt(ms + NORM_EPS) * gcol_ref[...] * (1.0 - LAM_INIT)
            o_ref[0, :, h * LANES:(h + 1) * LANES] = y.T.astype(o_ref.dtype)
        for g in range(FX_HEADS // 2):
            i0 = 2 * DA_HEADS + 2 * g
            y = jnp.concatenate([normalized(i0, FX_DIM), normalized(i0 + 1, FX_DIM)], axis=0)
            o_ref[0, :, GROUP_COLS + g * LANES:GROUP_COLS + (g + 1) * LANES] = y.T.astype(o_ref.dtype)


def _flash_kernel(qtab_ref, ktab_ref, *refs, T):
    t = pl.program_id(1)
    _flash_step(qtab_ref[t], ktab_ref[t], *refs, T)


def _flash_prompt(qT, katt, kx, vT, yq, lamv, gcol):
    B, _, _, S = qT.shape
    T = ATT_T
    n = S // T
    qtab = jnp.asarray([qi for qi in range(n) for _ in range(qi + 1)], jnp.int32)
    ktab = jnp.asarray([ki for qi in range(n) for ki in range(qi + 1)], jnp.int32)
    q_cols = lambda rows: pl.BlockSpec((1, rows[0], rows[1], T), lambda b, t, qt, kt: (b, 0, 0, qt[t]))
    in_specs = [
        q_cols((N_PAIRS, LANES)),
        pl.BlockSpec((1, N_PAIRS, T, LANES), lambda b, t, qt, kt: (b, 0, kt[t], 0)),
        pl.BlockSpec((1, 4, T, LANES), lambda b, t, qt, kt: (b, 0, kt[t], 0)),
        pl.BlockSpec((1, N_PAIRS, LANES, T), lambda b, t, qt, kt: (b, 0, 0, kt[t])),
        q_cols((FX_HEADS, AUX_ROWS)),
        pl.BlockSpec(lamv.shape, lambda b, t, qt, kt: (0, 0)),
        pl.BlockSpec(gcol.shape, lambda b, t, qt, kt: (0, 0)),
    ]
    return pl.pallas_call(
        functools.partial(_flash_kernel, T=T),
        grid_spec=pltpu.PrefetchScalarGridSpec(
            num_scalar_prefetch=2,
            grid=(B, n * (n + 1) // 2),
            in_specs=in_specs,
            out_specs=pl.BlockSpec((1, T, 2 * GROUP_COLS), lambda b, t, qt, kt: (b, qt[t], 0)),
            scratch_shapes=[
                pltpu.VMEM((2 * N_PAIRS, 1, T), F32),
                pltpu.VMEM((2 * N_PAIRS, DA_V_DIM + DEN_ROWS, T), F32),
                pltpu.VMEM((QK_AHEAD + 1, T, T), F32),
            ],
        ),
        out_shape=jax.ShapeDtypeStruct((B, S, 2 * GROUP_COLS), BF16),
        compiler_params=pltpu.CompilerParams(dimension_semantics=("arbitrary", "arbitrary"),
                                             vmem_limit_bytes=VMEM_LIMIT),
        name="flash_prompt",
    )(qtab, ktab, qT, katt, kx, vT, yq, lamv, gcol)


def _ff_chunks(n_ff):
    assert n_ff % MXU_TILE == 0
    tiles = n_ff // MXU_TILE
    first = (tiles + 1) // 2 * MXU_TILE
    return ((0, first), (first, n_ff - first))


def _gated_chunk(h, g, gm1, gm2, wu_ref, wd_ref, cw_ref, cb_ref, cols):
    u = _dot(h, wu_ref[:, cols])
    cv = cb_ref[:, cols] + cw_ref[0:1, cols] * gm2 + cw_ref[1:2, cols] * gm1 + cw_ref[2:3, cols] * g
    act = jax.nn.gelu(cv, approximate=True) * u
    return _dot(act.astype(BF16), wd_ref[cols, :])


def _post_prompt_kernel(x_ref, o_ref, wo_ref, g1_ref, g2_ref, wg_ref, wu_ref, cw_ref, cb_ref, wd_ref, g3_ref,
                        y_ref, st_ref, gext_ref, *, T):
    s = pl.program_id(1)
    n_ff = wg_ref.shape[1]

    @pl.when(s == 0)
    def _():
        gext_ref[0:8, :] = jnp.zeros((8, n_ff), F32)

    x1 = x_ref[0] + _rms(_dot(o_ref[0], wo_ref[...]), g1_ref[...])
    h = _rms(x1, g2_ref[...]).astype(BF16)
    out = jnp.zeros_like(x1)
    for c0, width in _ff_chunks(n_ff):
        cols = pl.ds(c0, width)
        g = _dot(h, wg_ref[:, cols])
        gext_ref[8:T + 8, cols] = g
        gm1 = gext_ref[7:T + 7, cols]
        gm2 = gext_ref[6:T + 6, cols]
        out = out + _gated_chunk(h, g, gm1, gm2, wu_ref, wd_ref, cw_ref, cb_ref, cols)
    last = gext_ref[T + 6:T + 8, :]
    gext_ref[6:8, :] = last
    y_ref[0] = x1 + _rms(out, g3_ref[...])

    @pl.when(s == pl.num_programs(1) - 1)
    def _():
        st_ref[0] = last


def _post_prompt(x, o, wo, g1, g2, wg, wu, cw, cb, wd, g3):
    B, S, D = x.shape
    T = FFN_T
    n_ff = wg.shape[1]
    tok = lambda w: pl.BlockSpec((1, T, w), lambda b, s: (b, s, 0))
    in_specs = [tok(D), tok(D)] + [_const_spec(a.shape) for a in (wo, g1, g2, wg, wu, cw, cb, wd, g3)]
    return pl.pallas_call(
        functools.partial(_post_prompt_kernel, T=T),
        grid=(B, S // T),
        in_specs=in_specs,
        out_specs=(tok(D), pl.BlockSpec((1, CONV_W - 1, n_ff), lambda b, s: (b, 0, 0))),
        out_shape=(jax.ShapeDtypeStruct((B, S, D), F32), jax.ShapeDtypeStruct((B, CONV_W - 1, n_ff), F32)),
        scratch_shapes=[pltpu.VMEM((T + 8, n_ff), F32)],
        compiler_params=pltpu.CompilerParams(dimension_semantics=("arbitrary", "arbitrary"),
                                             vmem_limit_bytes=VMEM_LIMIT),
        name="post_prompt",
    )(x, o, wo, g1, g2, wg, wu, cw, cb, wd, g3)


def _proj_sample_kernel(x_ref, g_ref, wq_ref, wk_ref, wv_ref, wf_ref, bf_ref, inv_ref, pos_ref,
                        qa_ref, qf_ref, ka_ref, va_ref, kf_ref, vf_ref, lf_ref):
    n = x_ref.shape[0]
    lane = lax.broadcasted_iota(jnp.int32, (n, LANES), 1)
    first_half = (lane % 64) < 32
    cos, sin = _rope_tables(jnp.broadcast_to(pos_ref[...], (n, LANES)), inv_ref[...])
    sin_signed = jnp.where(first_half, -sin, sin)
    h = _rms(x_ref[...], g_ref[...]).astype(BF16)

    zq = _dot(h, wq_ref[...])
    zk = _dot(h, wk_ref[...])
    for p in range(4):
        cols = slice(p * LANES, (p + 1) * LANES)
        qa_ref[:, cols] = _rope_chunk(zq[:, cols], cos, sin_signed, first_half) * QK_SCALE
        ka_ref[:, cols] = _rope_chunk(zk[:, cols], cos, sin_signed, first_half)
    qf_ref[...] = zq[:, GROUP_COLS:] * QK_SCALE
    kf_ref[...] = zk[:, GROUP_COLS:]
    zv = _dot(h, wv_ref[...])
    va_ref[...] = zv[:, :GROUP_COLS]
    vf_ref[...] = zv[:, GROUP_COLS:]
    lf = _log_sigmoid(_dot(h, wf_ref[...]) + bf_ref[...])
    lf_ref[...] = jnp.where(lane < FX_HEADS, lf, 0.0)


def _proj_sample(x, g, wq, wk, wv, wf, bf, inv, pos):
    n = x.shape[0]
    row = lambda w: jax.ShapeDtypeStruct((n, w), F32)
    return pl.pallas_call(
        _proj_sample_kernel,
        out_shape=(row(GROUP_COLS),) * 6 + (row(LANES),),
        compiler_params=pltpu.CompilerParams(vmem_limit_bytes=VMEM_LIMIT),
        name="proj_sample",
    )(x, g, wq, wk, wv, wf, bf, inv, pos)


def _forget_bias_kernel(pt_ref, pool_ref, lfnew_ref, upper_ref, later_ref, out_ref, *, n_pages):
    b = pl.program_id(0)
    x = jnp.concatenate([pool_ref[pt_ref[b, i]] for i in range(n_pages)], axis=0)
    a1, a2, a3 = _split3(x)
    within = (_dot(a1, upper_ref[...]) + _dot(a2, upper_ref[...])) + _dot(a3, upper_ref[...])
    page_total = jnp.broadcast_to(within[:, PAGE_SIZE - 1:PAGE_SIZE], x.shape)
    t1, t2, t3 = _split3(page_total)
    suffix = (_dot(later_ref[...], t1) + _dot(later_ref[...], t2)) + _dot(later_ref[...], t3)
    lf_new = jnp.concatenate([lfnew_ref[0]] * n_pages, axis=0)
    out_ref[0] = (suffix + lf_new) - within


def _forget_bias(page_table, cache_lf, lf_new, upper, later):
    n, n_pages = page_table.shape
    rows = n_pages * FX_HEADS

    in_specs = [
        _const_spec(cache_lf.shape),
        pl.BlockSpec((1, FX_HEADS, LANES), lambda b, pt: (b, 0, 0)),
        pl.BlockSpec(upper.shape, lambda b, pt: (0, 0)),
        pl.BlockSpec(later.shape, lambda b, pt: (0, 0)),
    ]
    return pl.pallas_call(
        functools.partial(_forget_bias_kernel, n_pages=n_pages),
        grid_spec=pltpu.PrefetchScalarGridSpec(
            num_scalar_prefetch=1,
            grid=(n,),
            in_specs=in_specs,
            out_specs=pl.BlockSpec((1, rows, PAGE_SIZE), lambda b, pt: (b, 0, 0)),
        ),
        out_shape=jax.ShapeDtypeStruct((n, rows, PAGE_SIZE), F32),
        compiler_params=pltpu.CompilerParams(dimension_semantics=("arbitrary",), vmem_limit_bytes=VMEM_LIMIT),
        name="forget_bias",
    )(page_table, cache_lf, lf_new, upper, later)


def _decode_step(c, n_chunks, refs, oa_ref, of_ref, m_ref, l_ref, accd_ref, accf_ref, pps):
    kd_refs, vd_refs = refs[0:pps], refs[pps:2 * pps]
    kf_refs, vf_refs = refs[2 * pps:3 * pps], refs[3 * pps:4 * pps]
    bias_ref, qa_ref, qf_ref, kan_ref, van_ref, kfn_ref, vfn_ref, lamv_ref = refs[4 * pps:]
    nt = (((1,), (1,)), ((), ()))

    lane4 = lax.broadcasted_iota(jnp.int32, (DA_HEADS, LANES), 1)
    q4 = qa_ref[0]
    qd = jnp.concatenate([jnp.where(lane4 < DA_QK_DIM, q4, 0.0), jnp.where(lane4 >= DA_QK_DIM, q4, 0.0)], axis=0)
    row = lax.broadcasted_iota(jnp.int32, (FX_HEADS, GROUP_COLS), 0)
    col = lax.broadcasted_iota(jnp.int32, (FX_HEADS, GROUP_COLS), 1)
    own_f = col // FX_DIM == row
    qf = jnp.where(own_f, jnp.broadcast_to(qf_ref[0], (FX_HEADS, GROUP_COLS)), 0.0)
    own_d = col % DA_HEADS == row % DA_HEADS

    @pl.when(c == 0)
    def _():
        m_ref[...] = jnp.full_like(m_ref, NEG_BIG)
        l_ref[...] = jnp.zeros_like(l_ref)
        accd_ref[...] = jnp.zeros_like(accd_ref)
        accf_ref[...] = jnp.zeros_like(accf_ref)

    def online(grp, s):
        m_old = m_ref[grp][:, 0:1]
        m_new = jnp.maximum(m_old, jnp.max(s, axis=1, keepdims=True))
        alpha = jnp.exp(m_old - m_new)
        p = jnp.exp(s - m_new)
        l_new = alpha * l_ref[grp][:, 0:1] + jnp.sum(p, axis=1, keepdims=True)
        m_ref[grp] = jnp.broadcast_to(m_new, (8, LANES))
        l_ref[grp] = jnp.broadcast_to(l_new, (8, LANES))
        return alpha, p.astype(BF16)

    qdb = qd.astype(BF16)
    s = jnp.concatenate(
        [jnp.where(own_d, lax.dot_general(qdb, kd_refs[i][0].astype(BF16), nt, preferred_element_type=F32), NEG_BIG)
         for i in range(pps)], axis=1)
    alpha, p = online(0, s)
    w = GROUP_COLS
    pv = _dot(p[:, 0:w], vd_refs[0][0].astype(BF16))
    for i in range(1, pps):
        pv = pv + _dot(p[:, i * w:(i + 1) * w], vd_refs[i][0].astype(BF16))
    accd_ref[...] = alpha * accd_ref[...] + pv

    qfb = qf.astype(BF16)
    s = jnp.concatenate([_dot(qfb, kf_refs[i][0].astype(BF16)) + bias_ref[0, i] for i in range(pps)], axis=1)
    alpha, p = online(1, s)
    w = PAGE_SIZE
    pv = lax.dot_general(p[:, 0:w], vf_refs[0][0].astype(BF16), nt, preferred_element_type=F32)
    for i in range(1, pps):
        pv = pv + lax.dot_general(p[:, i * w:(i + 1) * w], vf_refs[i][0].astype(BF16), nt,
                                  preferred_element_type=F32)
    accf_ref[...] = alpha * accf_ref[...] + pv

    @pl.when(c == n_chunks - 1)
    def _():
        def finish(grp, q, k_new, v_new, acc):
            s = jnp.sum(q * k_new, axis=1, keepdims=True)
            m_old = m_ref[grp][:, 0:1]
            m_new = jnp.maximum(m_old, s)
            alpha = jnp.exp(m_old - m_new)
            p = jnp.exp(s - m_new)
            l_new = alpha * l_ref[grp][:, 0:1] + p
            return (alpha * acc + p * v_new) * (1.0 / l_new)

        twice = lambda a: jnp.concatenate([a, a], axis=0)
        od = finish(0, qd, twice(kan_ref[0]), twice(van_ref[0]), accd_ref[...])
        oa_ref[0] = od[0:DA_HEADS] - _lambda_value(lamv_ref) * od[DA_HEADS:]
        of = finish(1, qf, kfn_ref[0], vfn_ref[0], accf_ref[...])
        of_ref[0] = jnp.sum(jnp.where(own_f, of, 0.0), axis=0, keepdims=True)


def _decode_kernel(pt_ref, *refs, pps):
    del pt_ref
    n_in = 4 * pps + 8
    _decode_step(pl.program_id(1), pl.num_programs(1), refs[:n_in], *refs[n_in:], pps)


def _decode_attention(page_table, cdk, cdv, cfk, cfv, bias, qa, qf, ka_new, va_new, kf_new, vf_new, lamv):
    n, n_pages = page_table.shape
    pps = PAGES_PER_STEP
    steps = n_pages // pps

    def page_spec(i):
        return pl.BlockSpec((1, GROUP_COLS, PAGE_SIZE), lambda b, c, pt: (pt[b, c * pps + i], 0, 0))

    row_spec = pl.BlockSpec((1, 1, GROUP_COLS), lambda b, c, pt: (b, 0, 0))
    head_spec = pl.BlockSpec((1, DA_HEADS, LANES), lambda b, c, pt: (b, 0, 0))
    in_specs = ([page_spec(i) for i in range(pps)] * 4
                + [pl.BlockSpec((1, pps, FX_HEADS, PAGE_SIZE), lambda b, c, pt: (b, c, 0, 0))]
                + [head_spec, row_spec, head_spec, head_spec, row_spec, row_spec]
                + [pl.BlockSpec(lamv.shape, lambda b, c, pt: (0, 0))])
    caches = [cdk] * pps + [cdv] * pps + [cfk] * pps + [cfv] * pps
    return pl.pallas_call(
        functools.partial(_decode_kernel, pps=pps),
        grid_spec=pltpu.PrefetchScalarGridSpec(
            num_scalar_prefetch=1,
            grid=(n, steps),
            in_specs=in_specs,
            out_specs=(head_spec, row_spec),
            scratch_shapes=[pltpu.VMEM((2, 8, LANES), F32), pltpu.VMEM((2, 8, LANES), F32),
                            pltpu.VMEM((8, LANES), F32), pltpu.VMEM((8, GROUP_COLS), F32)],
        ),
        out_shape=(jax.ShapeDtypeStruct((n, DA_HEADS, LANES), F32), jax.ShapeDtypeStruct((n, 1, GROUP_COLS), F32)),
        compiler_params=pltpu.CompilerParams(dimension_semantics=("arbitrary", "arbitrary"),
                                             vmem_limit_bytes=VMEM_LIMIT),
        name="decode_attention",
    )(page_table, *caches, bias, qa, qf, ka_new, va_new, kf_new, vf_new, lamv)


def _post_sample_kernel(x_ref, oa_ref, of_ref, sg_ref, wo_ref, g1_ref, g2_ref, wg_ref, wu_ref, cw_ref, cb_ref,
                        wd_ref, g3_ref, p0_ref, p1_ref, y_ref, gate_ref):
    n_ff = wg_ref.shape[1]
    heads = [_rms(oa_ref[:, h * LANES:(h + 1) * LANES], sg_ref[...]) * (1.0 - LAM_INIT) for h in range(DA_HEADS)]
    o = jnp.concatenate(heads + [of_ref[...]], axis=1).astype(BF16)
    x1 = x_ref[...] + _rms(_dot(o, wo_ref[...]), g1_ref[...])
    h = _rms(x1, g2_ref[...]).astype(BF16)
    out = jnp.zeros_like(x1)
    for c0, width in _ff_chunks(n_ff):
        cols = pl.ds(c0, width)
        g = _dot(h, wg_ref[:, cols])
        gate_ref[:, cols] = g
        out = out + _gated_chunk(h, g, p1_ref[:, cols], p0_ref[:, cols], wu_ref, wd_ref, cw_ref, cb_ref, cols)
    y_ref[...] = x1 + _rms(out, g3_ref[...])


def _post_sample(x, oa, of, sg, wo, g1, g2, wg, wu, cw, cb, wd, g3, p0, p1):
    n, d = x.shape
    return pl.pallas_call(
        _post_sample_kernel,
        out_shape=(jax.ShapeDtypeStruct((n, d), F32), jax.ShapeDtypeStruct((n, wg.shape[1]), F32)),
        compiler_params=pltpu.CompilerParams(vmem_limit_bytes=VMEM_LIMIT),
        name="post_sample",
    )(x, oa, of, sg, wo, g1, g2, wg, wu, cw, cb, wd, g3, p0, p1)


def kernel(x_prompt, x_sample, cache_diff_k, cache_diff_v, cache_fox_k, cache_fox_v, cache_fox_logf, state_ffn_conv, page_table, attn_pre_g, w_in, b_f, lam_q1, lam_k1, lam_q2, lam_k2, subln_g, w_o, attn_post_g, ffn_pre_g, w_gate, w_up, conv_w, conv_b, w_down, ffn_post_g):
    B, S, D = x_prompt.shape
    DB, DS, _ = x_sample.shape
    assert DS == 1 and w_in.shape[0] == 1, "one layer, one new token per sample"
    n_pool = cache_diff_k.shape[1]
    n_pages = page_table.shape[1]
    past_len = n_pages * PAGE_SIZE
    l = 0

    w = w_in[l]
    sec = lambda i: w[:, i * GROUP_COLS:(i + 1) * GROUP_COLS]
    wq = jnp.concatenate([sec(0), sec(3)], axis=1).astype(BF16)
    wk = jnp.concatenate([sec(1), sec(4)], axis=1).astype(BF16)
    wv = jnp.concatenate([sec(2), sec(5)], axis=1).astype(BF16)
    wf = jnp.pad(w[:, 6 * GROUP_COLS:], ((0, 0), (0, LANES - FX_HEADS))).astype(BF16)
    bf = jnp.pad(b_f[l], (0, LANES - FX_HEADS)).reshape(1, LANES)
    wo, wg, wu, wd = (a[l].astype(BF16) for a in (w_o, w_gate, w_up, w_down))
    row = lambda a: a[l].reshape(1, -1)
    g_pre, g_post, g_ffn, g_out, sg, cb = (row(a) for a in (attn_pre_g, attn_post_g, ffn_pre_g, ffn_post_g, subln_g, conv_b))
    cw = conv_w[l]
    half = DA_QK_DIM // 2
    inv = ROPE_THETA ** (-jnp.arange(half, dtype=F32) / half)
    inv = jnp.tile(inv, LANES // half).reshape(1, LANES)
    tri = jnp.tril(jnp.ones((CUM_T, CUM_T), BF16))
    lamv = jnp.stack([lam_q1[l], lam_k1[l], lam_q2[l], lam_k2[l]])

    sel = np.zeros((3, LANES, 4 * LANES), np.float32)
    for hd in range(FX_HEADS):
        for term in range(3):
            sel[term, hd, (hd // 2) * LANES + AUX_KEY[hd % 2] + term] = 1.0
    ka4, va4, kfT, vfT, lfT, katt, qT, vT, kx, yq = _proj_prompt(x_prompt, g_pre, wq, wk, wv, wf, bf, inv, tri,
                                                                 jnp.asarray(sel, BF16))
    heads_last = lambda t: jnp.transpose(t.reshape(B, FX_HEADS, FX_DIM, S), (0, 3, 1, 2))
    kf, vf, logf = heads_last(kfT), heads_last(vfT), jnp.transpose(lfT, (0, 2, 1))

    pos = jnp.full((1, LANES), past_len, F32)
    qa_s, qf_s, ka_s, va_s, kf_s, vf_s, lf_s = _proj_sample(x_sample[:, 0, :], g_pre, wq, wk, wv, wf, bf, inv, pos)
    rows_d = lambda c: c[l].reshape(n_pool, PAGE_SIZE * DA_HEADS, 2 * DA_QK_DIM)
    rows_f = lambda c: jnp.transpose(c[l], (0, 2, 3, 1)).reshape(n_pool, GROUP_COLS, PAGE_SIZE)
    lf_pages = jnp.transpose(cache_fox_logf[l], (0, 2, 1))
    key_i = jnp.arange(PAGE_SIZE)
    upper = (key_i[:, None] <= key_i[None, :]).astype(BF16)
    row_i = jnp.arange(n_pages * FX_HEADS)
    later = ((row_i[:, None] % FX_HEADS == row_i[None, :] % FX_HEADS)
             & (row_i[None, :] // FX_HEADS >= row_i[:, None] // FX_HEADS)).astype(BF16)
    lf_new = jnp.broadcast_to(lf_s[:, :FX_HEADS, None], (DB, FX_HEADS, LANES))
    bias = _forget_bias(page_table, lf_pages, lf_new, upper, later).reshape(DB, n_pages, FX_HEADS, PAGE_SIZE)
    r3 = lambda a: a.reshape(DB, 1, GROUP_COLS)
    r4 = lambda a: a.reshape(DB, DA_HEADS, LANES)
    oa_s, of_s = _decode_attention(page_table, rows_d(cache_diff_k), rows_d(cache_diff_v), rows_f(cache_fox_k),
                                   rows_f(cache_fox_v), bias, r4(qa_s), r3(qf_s), r4(ka_s), r4(va_s), r3(kf_s),
                                   r3(vf_s), lamv)
    o = _flash_prompt(qT, katt, kx, vT, yq, lamv, sg.reshape(DA_V_DIM, 1))
    y_prompt, conv_p = _post_prompt(x_prompt, o, wo, g_post, g_ffn, wg, wu, cw, cb, wd, g_out)
    prev = state_ffn_conv[l]
    y_sample, gate_s = _post_sample(x_sample[:, 0, :], oa_s.reshape(DB, GROUP_COLS), of_s[:, 0, :], sg, wo, g_post,
                                    g_ffn, wg, wu, cw, cb, wd, g_out, prev[:, 0, :], prev[:, 1, :])
    conv_s = jnp.stack([prev[:, 1, :], gate_s], axis=1)

    lead = lambda a, shape: a.reshape((1,) + shape)
    return (
        y_prompt,
        y_sample.reshape(DB, DS, D),
        lead(ka4, (B, S, DA_HEADS, 2 * DA_QK_DIM)),
        lead(va4, (B, S, DA_HEADS, DA_V_DIM)),
        lead(kf, (B, S, FX_HEADS, FX_DIM)),
        lead(vf, (B, S, FX_HEADS, FX_DIM)),
        lead(logf, (B, S, FX_HEADS)),
        lead(conv_p, (B, CONV_W - 1, conv_p.shape[-1])),
        lead(ka_s, (DB, DS, DA_HEADS, 2 * DA_QK_DIM)),
        lead(va_s, (DB, DS, DA_HEADS, DA_V_DIM)),
        lead(kf_s, (DB, DS, FX_HEADS, FX_DIM)),
        lead(vf_s, (DB, DS, FX_HEADS, FX_DIM)),
        lead(lf_s[:, :FX_HEADS], (DB, DS, FX_HEADS)),
        lead(conv_s, (DB, CONV_W - 1, conv_s.shape[-1])),
    )
```

```python
import functools
import math

import jax
import jax.numpy as jnp
import numpy as np
from jax import lax
from jax.experimental import pallas as pl
from jax.experimental.pallas import tpu as pltpu

F32 = jnp.float32
BF16 = jnp.bfloat16

DA_HEADS = 4
DA_QK_DIM = 64
DA_V_DIM = 128
FX_HEADS = 8
FX_DIM = 64
GROUP_COLS = 512
CONV_W = 3
ROPE_THETA = 10000.0
NORM_EPS = 1e-6
PAGE_SIZE = 128
LAM_INIT = 0.8 - 0.6 * math.exp(-0.3 * 0)
QK_SCALE = 0.125

LANES = 128
MXU_TILE = 256
N_PAIRS = 8
NEG_BIG = -1e30
LOG2E = 1.4426950408889634
AUX_ROWS = 16
AUX_KEY = (0, 3)
AUX_ONE = 6
DEN_ROWS = 16

PROJ_T = 512
ATT_T = 512
FFN_T = 512
QK_AHEAD = 2
PAGES_PER_STEP = 16
BIAS_SAMPLES_PER_STEP = 4
VMEM_LIMIT = 56 * 1024 * 1024


def _rms(x, g):
    return x * lax.rsqrt(jnp.mean(x * x, axis=-1, keepdims=True) + NORM_EPS) * g


def _dot(a, b):
    return jnp.dot(a, b, preferred_element_type=F32)


def _split3(x):
    a1 = x.astype(BF16)
    r1 = x - a1.astype(F32)
    a2 = r1.astype(BF16)
    a3 = (r1 - a2.astype(F32)).astype(BF16)
    return a1, a2, a3


def _tri_cumsum(tri, x):
    a1, a2, a3 = _split3(x)
    return (_dot(tri, a1) + _dot(tri, a2)) + _dot(tri, a3)


def _log_sigmoid(z):
    return jnp.minimum(z, 0.0) - jnp.log1p(jnp.exp(-jnp.abs(z)))


def _rope_tables(pos, inv):
    ang = pos * inv
    return jnp.cos(ang), jnp.sin(ang)


def _rope_chunk(z, cos, sin_signed, first_half):
    swapped = jnp.where(first_half, pltpu.roll(z, LANES - 32, 1), pltpu.roll(z, 32, 1))
    return z * cos + swapped * sin_signed


def _proj_prompt_kernel(x_ref, g_ref, wq_ref, wk_ref, wv_ref, wf_ref, bf_ref, inv_ref, tri_ref, sel_ref,
                        ka_ref, va_ref, kfT_ref, vfT_ref, lfT_ref, katt_ref, qT_ref, vT_ref, kx_ref, yq_ref,
                        carry_ref, cosb_ref, sinb_ref, *, T):
    b = pl.program_id(0)
    s = pl.program_id(1)
    lane = lax.broadcasted_iota(jnp.int32, (T, LANES), 1)
    first_half = (lane % 64) < 32

    @pl.when((b == 0) & (s == 0))
    def _():
        row = lax.broadcasted_iota(jnp.int32, (T, LANES), 0).astype(F32)
        cosb, sinb = _rope_tables(row, inv_ref[...])
        cosb_ref[...] = cosb
        sinb_ref[...] = sinb

    @pl.when(s == 0)
    def _():
        carry_ref[...] = jnp.zeros_like(carry_ref)

    base = jnp.full((8, LANES), s * T, jnp.int32).astype(F32)
    ca, sa = _rope_tables(base, inv_ref[...])
    ca, sa = ca[0:1], sa[0:1]
    cosb, sinb = cosb_ref[...], sinb_ref[...]
    cos = ca * cosb - sa * sinb
    sin = sa * cosb + ca * sinb
    sin_signed = jnp.where(first_half, -sin, sin)

    h = _rms(x_ref[0], g_ref[...]).astype(BF16)

    zf = _dot(h, wf_ref[...]) + bf_ref[...]

    zq = _dot(h, wq_ref[...])
    for p in range(4):
        q = _rope_chunk(zq[:, p * LANES:(p + 1) * LANES], cos, sin_signed, first_half)
        qT_ref[0, p] = (q * (QK_SCALE * LOG2E)).T.astype(BF16)
    for p in range(4):
        q = zq[:, GROUP_COLS + p * LANES:GROUP_COLS + (p + 1) * LANES]
        qT_ref[0, 4 + p] = (q * (QK_SCALE * LOG2E)).T.astype(BF16)

    lf = jnp.where(lane < FX_HEADS, _log_sigmoid(zf), 0.0)
    lfT_ref[0] = lf.T[:FX_HEADS, :]
    cs = _tri_cumsum(tri_ref[...], lf) + carry_ref[0:1, :]
    carry_ref[0:1, :] = cs[T - 1:T, :]

    zk = _dot(h, wk_ref[...])
    for p in range(4):
        k = _rope_chunk(zk[:, p * LANES:(p + 1) * LANES], cos, sin_signed, first_half)
        ka_ref[0, pl.ds(p, T, stride=DA_HEADS), :] = k
        katt_ref[0, p] = k.astype(BF16)
    for p in range(4):
        kf = zk[:, GROUP_COLS + p * LANES:GROUP_COLS + (p + 1) * LANES]
        kfT_ref[0, p * LANES:(p + 1) * LANES, :] = kf.T
        katt_ref[0, 4 + p] = kf.astype(BF16)

    cs2 = cs * LOG2E
    k1, k2, k3 = _split3(cs2)
    aux = (_dot(k1, sel_ref[0]) + _dot(k2, sel_ref[1])) + _dot(k3, sel_ref[2])
    lane4 = lax.broadcasted_iota(jnp.int32, aux.shape, 1) % LANES
    aux = jnp.where((lane4 >= AUX_ONE) & (lane4 < AUX_ONE + 3), 1.0, aux)
    for g in range(4):
        kx_ref[0, g] = aux[:, g * LANES:(g + 1) * LANES].astype(BF16)
    q1, q2, q3 = (t.astype(F32) for t in _split3(cs2.T))
    r = lax.broadcasted_iota(jnp.int32, (AUX_ROWS, T), 0)
    for hd in range(FX_HEADS):
        key_rows = (r >= AUX_KEY[hd % 2]) & (r < AUX_KEY[hd % 2] + 3)
        blk = jnp.where(key_rows, -1.0, 0.0)
        for t, qt in enumerate((q1, q2, q3)):
            blk = jnp.where(r == AUX_ONE + t, jnp.broadcast_to(qt[hd:hd + 1], (AUX_ROWS, T)), blk)
        yq_ref[0, hd] = blk.astype(BF16)

    zv = _dot(h, wv_ref[...])
    for p in range(4):
        v = zv[:, p * LANES:(p + 1) * LANES]
        va_ref[0, pl.ds(p, T, stride=DA_HEADS), :] = v
        vT_ref[0, p] = v.T.astype(BF16)
    for p in range(4):
        vT = zv[:, GROUP_COLS + p * LANES:GROUP_COLS + (p + 1) * LANES].T
        vfT_ref[0, p * LANES:(p + 1) * LANES, :] = vT
        vT_ref[0, 4 + p] = vT.astype(BF16)


def _const_spec(shape):
    n = len(shape)
    return pl.BlockSpec(shape, lambda *_: (0,) * n, pipeline_mode=pl.Buffered(1))


def _proj_prompt(x, g, wq, wk, wv, wf, bf, inv, tri, sel):
    B, S, D = x.shape
    T = PROJ_T
    tok = lambda w: pl.BlockSpec((1, T, w), lambda b, s: (b, s, 0))
    rows4 = pl.BlockSpec((1, T * DA_HEADS, LANES), lambda b, s: (b, s, 0))
    featT = lambda n: pl.BlockSpec((1, n, T), lambda b, s: (b, 0, s))
    out_shape = (
        jax.ShapeDtypeStruct((B, S * DA_HEADS, LANES), F32),
        jax.ShapeDtypeStruct((B, S * DA_HEADS, LANES), F32),
        jax.ShapeDtypeStruct((B, GROUP_COLS, S), F32),
        jax.ShapeDtypeStruct((B, GROUP_COLS, S), F32),
        jax.ShapeDtypeStruct((B, FX_HEADS, S), F32),
        jax.ShapeDtypeStruct((B, N_PAIRS, S, LANES), BF16),
        jax.ShapeDtypeStruct((B, N_PAIRS, LANES, S), BF16),
        jax.ShapeDtypeStruct((B, N_PAIRS, LANES, S), BF16),
        jax.ShapeDtypeStruct((B, 4, S, LANES), BF16),
        jax.ShapeDtypeStruct((B, FX_HEADS, AUX_ROWS, S), BF16),
    )
    out_specs = (
        rows4, rows4, featT(GROUP_COLS), featT(GROUP_COLS), featT(FX_HEADS),
        pl.BlockSpec((1, N_PAIRS, T, LANES), lambda b, s: (b, 0, s, 0)),
        pl.BlockSpec((1, N_PAIRS, LANES, T), lambda b, s: (b, 0, 0, s)),
        pl.BlockSpec((1, N_PAIRS, LANES, T), lambda b, s: (b, 0, 0, s)),
        pl.BlockSpec((1, 4, T, LANES), lambda b, s: (b, 0, s, 0)),
        pl.BlockSpec((1, FX_HEADS, AUX_ROWS, T), lambda b, s: (b, 0, 0, s)),
    )
    in_specs = [tok(D), _const_spec(g.shape), _const_spec(wq.shape), _const_spec(wk.shape),
                _const_spec(wv.shape), _const_spec(wf.shape), _const_spec(bf.shape),
                _const_spec(inv.shape), _const_spec(tri.shape), _const_spec(sel.shape)]
    return pl.pallas_call(
        functools.partial(_proj_prompt_kernel, T=T),
        grid=(B, S // T),
        in_specs=in_specs,
        out_specs=out_specs,
        out_shape=out_shape,
        scratch_shapes=[pltpu.VMEM((8, LANES), F32), pltpu.VMEM((T, LANES), F32), pltpu.VMEM((T, LANES), F32)],
        compiler_params=pltpu.CompilerParams(dimension_semantics=("arbitrary", "arbitrary"),
                                             vmem_limit_bytes=VMEM_LIMIT),
        name="proj_prompt",
    )(x, g, wq, wk, wv, wf, bf, inv, tri, sel)


def _lambda_value(lamv_ref):
    v = lamv_ref[...]
    d1 = jnp.sum(v[0:1] * v[1:2], axis=1, keepdims=True)
    d2 = jnp.sum(v[2:3] * v[3:4], axis=1, keepdims=True)
    return jnp.exp(d1) - jnp.exp(d2) + LAM_INIT


def _flash_step(qi, ki, qT_ref, k_ref, kx_ref, vT_ref, yq_ref, lamv_ref, gcol_ref, o_ref, m_ref, acc_ref,
                s_buf, T):
    @pl.when(ki == 0)
    def _():
        m_ref[...] = jnp.full_like(m_ref, NEG_BIG)
        acc_ref[...] = jnp.zeros_like(acc_ref)

    def step(masked):
        if masked:
            key_i = lax.broadcasted_iota(jnp.int32, (T, T), 0)
            qry_i = lax.broadcasted_iota(jnp.int32, (T, T), 1)
            valid = key_i <= qry_i
        zeros = jnp.zeros((64, T), BF16)
        aux_pad = jnp.zeros((LANES - AUX_ROWS, T), BF16)

        n_slots = s_buf.shape[0]

        def scores(i):
            pair, w = i // 2, i % 2
            qTp = qT_ref[0, pair]
            wq = jnp.concatenate([qTp[:64], zeros] if w == 0 else [zeros, qTp[64:]], axis=0)
            if pair < DA_HEADS:
                s = _dot(k_ref[0, pair], wq)
            else:
                lhs = jnp.concatenate([k_ref[0, pair], kx_ref[0, pair - DA_HEADS]], axis=1)
                s = _dot(lhs, jnp.concatenate([wq, yq_ref[0, i - 2 * DA_HEADS], aux_pad], axis=0))
            if masked:
                s = jnp.where(valid, s, NEG_BIG)
            s_buf[i % n_slots] = s
            return jnp.max(s, axis=0, keepdims=True)

        def online(i, tile_max):
            m_old = m_ref[i]
            m_new = jnp.maximum(m_old, tile_max)
            m_ref[i] = m_new
            return jnp.exp2(m_old - m_new), jnp.exp2(s_buf[i % n_slots] - m_new).astype(BF16)

        ones_blk = jnp.where(lax.broadcasted_iota(jnp.int32, (DEN_ROWS, T), 0) == 0, 1.0, 0.0).astype(BF16)

        n_soft = 2 * N_PAIRS
        tile_max = [scores(j) for j in range(QK_AHEAD)]
        for i in range(n_soft):
            if i + QK_AHEAD < n_soft:
                tile_max.append(scores(i + QK_AHEAD))
            alpha, p = online(i, tile_max.pop(0))
            pair = i // 2
            if pair < DA_HEADS:
                vals = vT_ref[0, pair]
            else:
                vals = vT_ref[0, pair, (i % 2) * FX_DIM:(i % 2 + 1) * FX_DIM, :]
            rows = pl.ds(0, vals.shape[0] + DEN_ROWS)
            acc_ref[i, rows, :] = alpha * acc_ref[i, rows, :] + _dot(jnp.concatenate([vals, ones_blk], axis=0), p)

    @pl.when(ki < qi)
    def _():
        step(False)

    @pl.when(ki == qi)
    def _():
        step(True)

        def normalized(i, n_rows):
            return acc_ref[i, 0:n_rows, :] * (1.0 / acc_ref[i, n_rows:n_rows + 1, :])

        lam = _lambda_value(lamv_ref)
        for h in range(DA_HEADS):
            o = normalized(2 * h, DA_V_DIM) - lam * normalized(2 * h + 1, DA_V_DIM)
            ms = jnp.mean(o * o, axis=0, keepdims=True)
            y = o * lax.rsqrt(ms + NORM_EPS) * gcol_ref[...] * (1.0 - LAM_INIT)
            o_ref[0, :, h * LANES:(h + 1) * LANES] = y.T.astype(o_ref.dtype)
        for g in range(FX_HEADS // 2):
            i0 = 2 * DA_HEADS + 2 * g
            y = jnp.concatenate([normalized(i0, FX_DIM), normalized(i0 + 1, FX_DIM)], axis=0)
            o_ref[0, :, GROUP_COLS + g * LANES:GROUP_COLS + (g + 1) * LANES] = y.T.astype(o_ref.dtype)


def _flash_kernel(qtab_ref, ktab_ref, *refs, T):
    t = pl.program_id(1)
    _flash_step(qtab_ref[t], ktab_ref[t], *refs, T)


def _flash_prompt(qT, katt, kx, vT, yq, lamv, gcol):
    B, _, _, S = qT.shape
    T = ATT_T
    n = S // T
    qtab = jnp.asarray([qi for qi in range(n) for _ in range(qi + 1)], jnp.int32)
    ktab = jnp.asarray([ki for qi in range(n) for ki in range(qi + 1)], jnp.int32)
    q_cols = lambda rows: pl.BlockSpec((1, rows[0], rows[1], T), lambda b, t, qt, kt: (b, 0, 0, qt[t]))
    in_specs = [
        q_cols((N_PAIRS, LANES)),
        pl.BlockSpec((1, N_PAIRS, T, LANES), lambda b, t, qt, kt: (b, 0, kt[t], 0)),
        pl.BlockSpec((1, 4, T, LANES), lambda b, t, qt, kt: (b, 0, kt[t], 0)),
        pl.BlockSpec((1, N_PAIRS, LANES, T), lambda b, t, qt, kt: (b, 0, 0, kt[t])),
        q_cols((FX_HEADS, AUX_ROWS)),
        pl.BlockSpec(lamv.shape, lambda b, t, qt, kt: (0, 0)),
        pl.BlockSpec(gcol.shape, lambda b, t, qt, kt: (0, 0)),
    ]
    return pl.pallas_call(
        functools.partial(_flash_kernel, T=T),
        grid_spec=pltpu.PrefetchScalarGridSpec(
            num_scalar_prefetch=2,
            grid=(B, n * (n + 1) // 2),
            in_specs=in_specs,
            out_specs=pl.BlockSpec((1, T, 2 * GROUP_COLS), lambda b, t, qt, kt: (b, qt[t], 0)),
            scratch_shapes=[
                pltpu.VMEM((2 * N_PAIRS, 1, T), F32),
                pltpu.VMEM((2 * N_PAIRS, DA_V_DIM + DEN_ROWS, T), F32),
                pltpu.VMEM((QK_AHEAD + 1, T, T), F32),
            ],
        ),
        out_shape=jax.ShapeDtypeStruct((B, S, 2 * GROUP_COLS), BF16),
        compiler_params=pltpu.CompilerParams(dimension_semantics=("arbitrary", "arbitrary"),
                                             vmem_limit_bytes=VMEM_LIMIT),
        name="flash_prompt",
    )(qtab, ktab, qT, katt, kx, vT, yq, lamv, gcol)


def _ff_chunks(n_ff):
    assert n_ff % MXU_TILE == 0
    tiles = n_ff // MXU_TILE
    first = (tiles + 1) // 2 * MXU_TILE
    return ((0, first), (first, n_ff - first))


def _gated_chunk(u, g, gm1, gm2, wd_ref, cw_ref, cb_ref, cols):
    cv = cb_ref[:, cols] + cw_ref[0:1, cols] * gm2 + cw_ref[1:2, cols] * gm1 + cw_ref[2:3, cols] * g
    act = jax.nn.gelu(cv, approximate=True) * u
    return _dot(act.astype(BF16), wd_ref[cols, :])


def _post_prompt_kernel(x_ref, o_ref, wo_ref, g1_ref, g2_ref, wg_ref, wu_ref, cw_ref, cb_ref, wd_ref, g3_ref,
                        y_ref, st_ref, gext_ref, *, T):
    s = pl.program_id(1)
    n_ff = wg_ref.shape[1]

    @pl.when(s == 0)
    def _():
        gext_ref[0:8, :] = jnp.zeros((8, n_ff), F32)

    x1 = x_ref[0] + _rms(_dot(o_ref[0], wo_ref[...]), g1_ref[...])
    h = _rms(x1, g2_ref[...]).astype(BF16)
    out = jnp.zeros_like(x1)
    for c0, width in _ff_chunks(n_ff):
        cols = pl.ds(c0, width)
        g = _dot(h, wg_ref[:, cols])
        gext_ref[8:T + 8, cols] = g
        gm1 = gext_ref[7:T + 7, cols]
        gm2 = gext_ref[6:T + 6, cols]
        u = _dot(h, wu_ref[:, cols])
        out = out + _gated_chunk(u, g, gm1, gm2, wd_ref, cw_ref, cb_ref, cols)
    last = gext_ref[T + 6:T + 8, :]
    gext_ref[6:8, :] = last
    y_ref[0] = x1 + _rms(out, g3_ref[...])

    @pl.when(s == pl.num_programs(1) - 1)
    def _():
        st_ref[0] = last


def _post_prompt(x, o, wo, g1, g2, wg, wu, cw, cb, wd, g3):
    B, S, D = x.shape
    T = FFN_T
    n_ff = wg.shape[1]
    tok = lambda w: pl.BlockSpec((1, T, w), lambda b, s: (b, s, 0))
    in_specs = [tok(D), tok(D)] + [_const_spec(a.shape) for a in (wo, g1, g2, wg, wu, cw, cb, wd, g3)]
    return pl.pallas_call(
        functools.partial(_post_prompt_kernel, T=T),
        grid=(B, S // T),
        in_specs=in_specs,
        out_specs=(tok(D), pl.BlockSpec((1, CONV_W - 1, n_ff), lambda b, s: (b, 0, 0))),
        out_shape=(jax.ShapeDtypeStruct((B, S, D), F32), jax.ShapeDtypeStruct((B, CONV_W - 1, n_ff), F32)),
        scratch_shapes=[pltpu.VMEM((T + 8, n_ff), F32)],
        compiler_params=pltpu.CompilerParams(dimension_semantics=("arbitrary", "arbitrary"),
                                             vmem_limit_bytes=VMEM_LIMIT),
        name="post_prompt",
    )(x, o, wo, g1, g2, wg, wu, cw, cb, wd, g3)


def _proj_sample_kernel(x_ref, g_ref, wq_ref, wk_ref, wv_ref, wf_ref, bf_ref, inv_ref, pos_ref,
                        qa_ref, qf_ref, ka_ref, va_ref, kf_ref, vf_ref, lf_ref):
    n = x_ref.shape[0]
    lane = lax.broadcasted_iota(jnp.int32, (n, LANES), 1)
    first_half = (lane % 64) < 32
    cos, sin = _rope_tables(jnp.broadcast_to(pos_ref[...], (n, LANES)), inv_ref[...])
    sin_signed = jnp.where(first_half, -sin, sin)
    h = _rms(x_ref[...], g_ref[...]).astype(BF16)

    zq = _dot(h, wq_ref[...])
    zk = _dot(h, wk_ref[...])
    for p in range(4):
        cols = slice(p * LANES, (p + 1) * LANES)
        qa_ref[:, cols] = _rope_chunk(zq[:, cols], cos, sin_signed, first_half) * QK_SCALE
        ka_ref[:, cols] = _rope_chunk(zk[:, cols], cos, sin_signed, first_half)
    qf_ref[...] = zq[:, GROUP_COLS:] * QK_SCALE
    kf_ref[...] = zk[:, GROUP_COLS:]
    zv = _dot(h, wv_ref[...])
    va_ref[...] = zv[:, :GROUP_COLS]
    vf_ref[...] = zv[:, GROUP_COLS:]
    lf = _log_sigmoid(_dot(h, wf_ref[...]) + bf_ref[...])
    lf_ref[...] = jnp.where(lane < FX_HEADS, lf, 0.0)


def _proj_sample(x, g, wq, wk, wv, wf, bf, inv, pos):
    n = x.shape[0]
    row = lambda w: jax.ShapeDtypeStruct((n, w), F32)
    return pl.pallas_call(
        _proj_sample_kernel,
        out_shape=(row(GROUP_COLS),) * 6 + (row(LANES),),
        compiler_params=pltpu.CompilerParams(vmem_limit_bytes=VMEM_LIMIT),
        name="proj_sample",
    )(x, g, wq, wk, wv, wf, bf, inv, pos)


def _forget_bias_kernel(pt_ref, pool_ref, lfnew_ref, upper_ref, later_ref, out_ref, *, n_pages):
    for j in range(BIAS_SAMPLES_PER_STEP):
        b = pl.program_id(0) * BIAS_SAMPLES_PER_STEP + j
        x = jnp.concatenate([pool_ref[pt_ref[b, i]] for i in range(n_pages)], axis=0)
        a1, a2, a3 = _split3(x)
        within = (_dot(a1, upper_ref[...]) + _dot(a2, upper_ref[...])) + _dot(a3, upper_ref[...])
        page_total = jnp.broadcast_to(within[:, PAGE_SIZE - 1:PAGE_SIZE], x.shape)
        t1, t2, t3 = _split3(page_total)
        suffix = (_dot(later_ref[...], t1) + _dot(later_ref[...], t2)) + _dot(later_ref[...], t3)
        lf_new = jnp.concatenate([lfnew_ref[j]] * n_pages, axis=0)
        out_ref[j] = (suffix + lf_new) - within


def _forget_bias(page_table, cache_lf, lf_new, upper, later):
    n, n_pages = page_table.shape
    rows = n_pages * FX_HEADS
    per = BIAS_SAMPLES_PER_STEP
    assert n % per == 0

    in_specs = [
        _const_spec(cache_lf.shape),
        pl.BlockSpec((per, FX_HEADS, LANES), lambda b, pt: (b, 0, 0)),
        pl.BlockSpec(upper.shape, lambda b, pt: (0, 0)),
        pl.BlockSpec(later.shape, lambda b, pt: (0, 0)),
    ]
    return pl.pallas_call(
        functools.partial(_forget_bias_kernel, n_pages=n_pages),
        grid_spec=pltpu.PrefetchScalarGridSpec(
            num_scalar_prefetch=1,
            grid=(n // per,),
            in_specs=in_specs,
            out_specs=pl.BlockSpec((per, rows, PAGE_SIZE), lambda b, pt: (b, 0, 0)),
        ),
        out_shape=jax.ShapeDtypeStruct((n, rows, PAGE_SIZE), F32),
        compiler_params=pltpu.CompilerParams(dimension_semantics=("arbitrary",), vmem_limit_bytes=VMEM_LIMIT),
        name="forget_bias",
    )(page_table, cache_lf, lf_new, upper, later)


def _decode_step(c, n_chunks, refs, oa_ref, of_ref, m_ref, l_ref, accd_ref, accf_ref, pps):
    kd_refs, vd_refs = refs[0:pps], refs[pps:2 * pps]
    kf_refs, vf_refs = refs[2 * pps:3 * pps], refs[3 * pps:4 * pps]
    bias_ref, qa_ref, qf_ref, kan_ref, van_ref, kfn_ref, vfn_ref, lamv_ref = refs[4 * pps:]
    nt = (((1,), (1,)), ((), ()))

    lane4 = lax.broadcasted_iota(jnp.int32, (DA_HEADS, LANES), 1)
    q4 = qa_ref[0]
    qd = jnp.concatenate([jnp.where(lane4 < DA_QK_DIM, q4, 0.0), jnp.where(lane4 >= DA_QK_DIM, q4, 0.0)], axis=0)
    row = lax.broadcasted_iota(jnp.int32, (FX_HEADS, GROUP_COLS), 0)
    col = lax.broadcasted_iota(jnp.int32, (FX_HEADS, GROUP_COLS), 1)
    own_f = col // FX_DIM == row
    qf = jnp.where(own_f, jnp.broadcast_to(qf_ref[0], (FX_HEADS, GROUP_COLS)), 0.0)
    own_d = col % DA_HEADS == row % DA_HEADS

    @pl.when(c == 0)
    def _():
        m_ref[...] = jnp.full_like(m_ref, NEG_BIG)
        l_ref[...] = jnp.zeros_like(l_ref)
        accd_ref[...] = jnp.zeros_like(accd_ref)
        accf_ref[...] = jnp.zeros_like(accf_ref)

    def online(grp, s):
        m_old = m_ref[grp][:, 0:1]
        m_new = jnp.maximum(m_old, jnp.max(s, axis=1, keepdims=True))
        alpha = jnp.exp(m_old - m_new)
        p = jnp.exp(s - m_new)
        l_new = alpha * l_ref[grp][:, 0:1] + jnp.sum(p, axis=1, keepdims=True)
        m_ref[grp] = jnp.broadcast_to(m_new, (8, LANES))
        l_ref[grp] = jnp.broadcast_to(l_new, (8, LANES))
        return alpha, p.astype(BF16)

    qdb = qd.astype(BF16)
    s = jnp.concatenate(
        [jnp.where(own_d, lax.dot_general(qdb, kd_refs[i][0].astype(BF16), nt, preferred_element_type=F32), NEG_BIG)
         for i in range(pps)], axis=1)
    alpha, p = online(0, s)
    w = GROUP_COLS
    pv = _dot(p[:, 0:w], vd_refs[0][0].astype(BF16))
    for i in range(1, pps):
        pv = pv + _dot(p[:, i * w:(i + 1) * w], vd_refs[i][0].astype(BF16))
    accd_ref[...] = alpha * accd_ref[...] + pv

    qfb = qf.astype(BF16)
    s = jnp.concatenate([_dot(qfb, kf_refs[i][0].astype(BF16)) + bias_ref[0, i] for i in range(pps)], axis=1)
    alpha, p = online(1, s)
    w = PAGE_SIZE
    pv = lax.dot_general(p[:, 0:w], vf_refs[0][0].astype(BF16), nt, preferred_element_type=F32)
    for i in range(1, pps):
        pv = pv + lax.dot_general(p[:, i * w:(i + 1) * w], vf_refs[i][0].astype(BF16), nt,
                                  preferred_element_type=F32)
    accf_ref[...] = alpha * accf_ref[...] + pv

    @pl.when(c == n_chunks - 1)
    def _():
        def finish(grp, q, k_new, v_new, acc):
            s = jnp.sum(q * k_new, axis=1, keepdims=True)
            m_old = m_ref[grp][:, 0:1]
            m_new = jnp.maximum(m_old, s)
            alpha = jnp.exp(m_old - m_new)
            p = jnp.exp(s - m_new)
            l_new = alpha * l_ref[grp][:, 0:1] + p
            return (alpha * acc + p * v_new) * (1.0 / l_new)

        twice = lambda a: jnp.concatenate([a, a], axis=0)
        od = finish(0, qd, twice(kan_ref[0]), twice(van_ref[0]), accd_ref[...])
        oa_ref[0] = od[0:DA_HEADS] - _lambda_value(lamv_ref) * od[DA_HEADS:]
        of = finish(1, qf, kfn_ref[0], vfn_ref[0], accf_ref[...])
        of_ref[0] = jnp.sum(jnp.where(own_f, of, 0.0), axis=0, keepdims=True)


def _decode_kernel(pt_ref, *refs, pps):
    del pt_ref
    n_in = 4 * pps + 8
    _decode_step(pl.program_id(1), pl.num_programs(1), refs[:n_in], *refs[n_in:], pps)


def _decode_attention(page_table, cdk, cdv, cfk, cfv, bias, qa, qf, ka_new, va_new, kf_new, vf_new, lamv):
    n, n_pages = page_table.shape
    pps = PAGES_PER_STEP
    steps = n_pages // pps

    def page_spec(i):
        return pl.BlockSpec((1, GROUP_COLS, PAGE_SIZE), lambda b, c, pt: (pt[b, c * pps + i], 0, 0))

    row_spec = pl.BlockSpec((1, 1, GROUP_COLS), lambda b, c, pt: (b, 0, 0))
    head_spec = pl.BlockSpec((1, DA_HEADS, LANES), lambda b, c, pt: (b, 0, 0))
    in_specs = ([page_spec(i) for i in range(pps)] * 4
                + [pl.BlockSpec((1, pps, FX_HEADS, PAGE_SIZE), lambda b, c, pt: (b, c, 0, 0))]
                + [head_spec, row_spec, head_spec, head_spec, row_spec, row_spec]
                + [pl.BlockSpec(lamv.shape, lambda b, c, pt: (0, 0))])
    caches = [cdk] * pps + [cdv] * pps + [cfk] * pps + [cfv] * pps
    return pl.pallas_call(
        functools.partial(_decode_kernel, pps=pps),
        grid_spec=pltpu.PrefetchScalarGridSpec(
            num_scalar_prefetch=1,
            grid=(n, steps),
            in_specs=in_specs,
            out_specs=(head_spec, row_spec),
            scratch_shapes=[pltpu.VMEM((2, 8, LANES), F32), pltpu.VMEM((2, 8, LANES), F32),
                            pltpu.VMEM((8, LANES), F32), pltpu.VMEM((8, GROUP_COLS), F32)],
        ),
        out_shape=(jax.ShapeDtypeStruct((n, DA_HEADS, LANES), F32), jax.ShapeDtypeStruct((n, 1, GROUP_COLS), F32)),
        compiler_params=pltpu.CompilerParams(dimension_semantics=("arbitrary", "arbitrary"),
                                             vmem_limit_bytes=VMEM_LIMIT),
        name="decode_attention",
    )(page_table, *caches, bias, qa, qf, ka_new, va_new, kf_new, vf_new, lamv)


def _post_sample_kernel(x_ref, oa_ref, of_ref, sg_ref, wo_ref, g1_ref, g2_ref, wg_ref, wu_ref, cw_ref, cb_ref,
                        wd_ref, g3_ref, p0_ref, p1_ref, y_ref, gate_ref):
    n_ff = wg_ref.shape[1]
    heads = [_rms(oa_ref[:, h * LANES:(h + 1) * LANES], sg_ref[...]) * (1.0 - LAM_INIT) for h in range(DA_HEADS)]
    o = jnp.concatenate(heads + [of_ref[...]], axis=1).astype(BF16)
    x1 = x_ref[...] + _rms(_dot(o, wo_ref[...]), g1_ref[...])
    h = _rms(x1, g2_ref[...]).astype(BF16)
    out = jnp.zeros_like(x1)
    for c0, width in _ff_chunks(n_ff):
        cols = pl.ds(c0, width)
        g = _dot(h, wg_ref[:, cols])
        gate_ref[:, cols] = g
        u = _dot(h, wu_ref[:, cols])
        out = out + _gated_chunk(u, g, p1_ref[:, cols], p0_ref[:, cols], wd_ref, cw_ref, cb_ref, cols)
    y_ref[...] = x1 + _rms(out, g3_ref[...])


def _post_sample(x, oa, of, sg, wo, g1, g2, wg, wu, cw, cb, wd, g3, p0, p1):
    n, d = x.shape
    return pl.pallas_call(
        _post_sample_kernel,
        out_shape=(jax.ShapeDtypeStruct((n, d), F32), jax.ShapeDtypeStruct((n, wg.shape[1]), F32)),
        compiler_params=pltpu.CompilerParams(vmem_limit_bytes=VMEM_LIMIT),
        name="post_sample",
    )(x, oa, of, sg, wo, g1, g2, wg, wu, cw, cb, wd, g3, p0, p1)


def kernel(x_prompt, x_sample, cache_diff_k, cache_diff_v, cache_fox_k, cache_fox_v, cache_fox_logf, state_ffn_conv, page_table, attn_pre_g, w_in, b_f, lam_q1, lam_k1, lam_q2, lam_k2, subln_g, w_o, attn_post_g, ffn_pre_g, w_gate, w_up, conv_w, conv_b, w_down, ffn_post_g):
    B, S, D = x_prompt.shape
    DB, DS, _ = x_sample.shape
    assert DS == 1 and w_in.shape[0] == 1, "one layer, one new token per sample"
    n_pool = cache_diff_k.shape[1]
    n_pages = page_table.shape[1]
    past_len = n_pages * PAGE_SIZE
    l = 0

    w = w_in[l]
    sec = lambda i: w[:, i * GROUP_COLS:(i + 1) * GROUP_COLS]
    wq = jnp.concatenate([sec(0), sec(3)], axis=1).astype(BF16)
    wk = jnp.concatenate([sec(1), sec(4)], axis=1).astype(BF16)
    wv = jnp.concatenate([sec(2), sec(5)], axis=1).astype(BF16)
    wf = jnp.pad(w[:, 6 * GROUP_COLS:], ((0, 0), (0, LANES - FX_HEADS))).astype(BF16)
    bf = jnp.pad(b_f[l], (0, LANES - FX_HEADS)).reshape(1, LANES)
    wo, wg, wu, wd = (a[l].astype(BF16) for a in (w_o, w_gate, w_up, w_down))
    row = lambda a: a[l].reshape(1, -1)
    g_pre, g_post, g_ffn, g_out, sg, cb = (row(a) for a in (attn_pre_g, attn_post_g, ffn_pre_g, ffn_post_g, subln_g, conv_b))
    cw = conv_w[l]
    half = DA_QK_DIM // 2
    inv = ROPE_THETA ** (-jnp.arange(half, dtype=F32) / half)
    inv = jnp.tile(inv, LANES // half).reshape(1, LANES)
    tri = jnp.tril(jnp.ones((PROJ_T, PROJ_T), BF16))
    lamv = jnp.stack([lam_q1[l], lam_k1[l], lam_q2[l], lam_k2[l]])

    sel = np.zeros((3, LANES, 4 * LANES), np.float32)
    for hd in range(FX_HEADS):
        for term in range(3):
            sel[term, hd, (hd // 2) * LANES + AUX_KEY[hd % 2] + term] = 1.0
    ka4, va4, kfT, vfT, lfT, katt, qT, vT, kx, yq = _proj_prompt(x_prompt, g_pre, wq, wk, wv, wf, bf, inv, tri,
                                                                 jnp.asarray(sel, BF16))
    heads_last = lambda t: jnp.transpose(t.reshape(B, FX_HEADS, FX_DIM, S), (0, 3, 1, 2))
    kf, vf, logf = heads_last(kfT), heads_last(vfT), jnp.transpose(lfT, (0, 2, 1))

    pos = jnp.full((1, LANES), past_len, F32)
    qa_s, qf_s, ka_s, va_s, kf_s, vf_s, lf_s = _proj_sample(x_sample[:, 0, :], g_pre, wq, wk, wv, wf, bf, inv, pos)
    rows_d = lambda c: c[l].reshape(n_pool, PAGE_SIZE * DA_HEADS, 2 * DA_QK_DIM)
    rows_f = lambda c: jnp.transpose(c[l], (0, 2, 3, 1)).reshape(n_pool, GROUP_COLS, PAGE_SIZE)
    lf_pages = jnp.transpose(cache_fox_logf[l], (0, 2, 1))
    key_i = jnp.arange(PAGE_SIZE)
    upper = (key_i[:, None] <= key_i[None, :]).astype(BF16)
    row_i = jnp.arange(n_pages * FX_HEADS)
    later = ((row_i[:, None] % FX_HEADS == row_i[None, :] % FX_HEADS)
             & (row_i[None, :] // FX_HEADS >= row_i[:, None] // FX_HEADS)).astype(BF16)
    lf_new = jnp.broadcast_to(lf_s[:, :FX_HEADS, None], (DB, FX_HEADS, LANES))
    bias = _forget_bias(page_table, lf_pages, lf_new, upper, later).reshape(DB, n_pages, FX_HEADS, PAGE_SIZE)
    r3 = lambda a: a.reshape(DB, 1, GROUP_COLS)
    r4 = lambda a: a.reshape(DB, DA_HEADS, LANES)
    oa_s, of_s = _decode_attention(page_table, rows_d(cache_diff_k), rows_d(cache_diff_v), rows_f(cache_fox_k),
                                   rows_f(cache_fox_v), bias, r4(qa_s), r3(qf_s), r4(ka_s), r4(va_s), r3(kf_s),
                                   r3(vf_s), lamv)
    o = _flash_prompt(qT, katt, kx, vT, yq, lamv, sg.reshape(DA_V_DIM, 1))
    y_prompt, conv_p = _post_prompt(x_prompt, o, wo, g_post, g_ffn, wg, wu, cw, cb, wd, g_out)
    prev = state_ffn_conv[l]
    y_sample, gate_s = _post_sample(x_sample[:, 0, :], oa_s.reshape(DB, GROUP_COLS), of_s[:, 0, :], sg, wo, g_post,
                                    g_ffn, wg, wu, cw, cb, wd, g_out, prev[:, 0, :], prev[:, 1, :])
    conv_s = jnp.stack([prev[:, 1, :], gate_s], axis=1)

    lead = lambda a, shape: a.reshape((1,) + shape)
    return (
        y_prompt,
        y_sample.reshape(DB, DS, D),
        lead(ka4, (B, S, DA_HEADS, 2 * DA_QK_DIM)),
        lead(va4, (B, S, DA_HEADS, DA_V_DIM)),
        lead(kf, (B, S, FX_HEADS, FX_DIM)),
        lead(vf, (B, S, FX_HEADS, FX_DIM)),
        lead(logf, (B, S, FX_HEADS)),
        lead(conv_p, (B, CONV_W - 1, conv_p.shape[-1])),
        lead(ka_s, (DB, DS, DA_HEADS, 2 * DA_QK_DIM)),
        lead(va_s, (DB, DS, DA_HEADS, DA_V_DIM)),
        lead(kf_s, (DB, DS, FX_HEADS, FX_DIM)),
        lead(vf_s, (DB, DS, FX_HEADS, FX_DIM)),
        lead(lf_s[:, :FX_HEADS], (DB, DS, FX_HEADS)),
        lead(conv_s, (DB, CONV_W - 1, conv_s.shape[-1])),
    )
```

```python
import functools
import math

import jax
import jax.numpy as jnp
import numpy as np
from jax import lax
from jax.experimental import pallas as pl
from jax.experimental.pallas import tpu as pltpu

F32 = jnp.float32
BF16 = jnp.bfloat16

DA_HEADS = 4
DA_QK_DIM = 64
DA_V_DIM = 128
FX_HEADS = 8
FX_DIM = 64
GROUP_COLS = 512
CONV_W = 3
ROPE_THETA = 10000.0
NORM_EPS = 1e-6
PAGE_SIZE = 128
LAM_INIT = 0.8 - 0.6 * math.exp(-0.3 * 0)
QK_SCALE = 0.125

LANES = 128
MXU_TILE = 256
N_PAIRS = 8
NEG_BIG = -1e30
LOG2E = 1.4426950408889634
AUX_ROWS = 16
AUX_KEY = (0, 3)
AUX_ONE = 6
DEN_ROWS = 16

PROJ_T = 512
ATT_T = 512
FFN_T = 512
QK_AHEAD = 2
PAGES_PER_STEP = 16
BIAS_SAMPLES_PER_STEP = 4
VMEM_LIMIT = 56 * 1024 * 1024


def _rms(x, g):
    return x * lax.rsqrt(jnp.mean(x * x, axis=-1, keepdims=True) + NORM_EPS) * g


def _dot(a, b):
    return jnp.dot(a, b, preferred_element_type=F32)


def _split3(x):
    a1 = x.astype(BF16)
    r1 = x - a1.astype(F32)
    a2 = r1.astype(BF16)
    a3 = (r1 - a2.astype(F32)).astype(BF16)
    return a1, a2, a3


def _tri_cumsum(tri, x):
    a1, a2, a3 = _split3(x)
    return (_dot(tri, a1) + _dot(tri, a2)) + _dot(tri, a3)


def _log_sigmoid(z):
    return jnp.minimum(z, 0.0) - jnp.log1p(jnp.exp(-jnp.abs(z)))


def _rope_tables(pos, inv):
    ang = pos * inv
    return jnp.cos(ang), jnp.sin(ang)


def _rope_chunk(z, cos, sin_signed, first_half):
    swapped = jnp.where(first_half, pltpu.roll(z, LANES - 32, 1), pltpu.roll(z, 32, 1))
    return z * cos + swapped * sin_signed


def _proj_prompt_kernel(x_ref, g_ref, wq_ref, wk_ref, wv_ref, wf_ref, bf_ref, inv_ref, tri_ref, sel_ref,
                        ka_ref, va_ref, kfT_ref, vfT_ref, lfT_ref, katt_ref, qT_ref, vT_ref, kx_ref, yq_ref,
                        carry_ref, cosb_ref, sinb_ref, *, T):
    b = pl.program_id(0)
    s = pl.program_id(1)
    lane = lax.broadcasted_iota(jnp.int32, (T, LANES), 1)
    first_half = (lane % 64) < 32

    @pl.when((b == 0) & (s == 0))
    def _():
        row = lax.broadcasted_iota(jnp.int32, (T, LANES), 0).astype(F32)
        cosb, sinb = _rope_tables(row, inv_ref[...])
        cosb_ref[...] = cosb
        sinb_ref[...] = sinb

    @pl.when(s == 0)
    def _():
        carry_ref[...] = jnp.zeros_like(carry_ref)

    base = jnp.full((8, LANES), s * T, jnp.int32).astype(F32)
    ca, sa = _rope_tables(base, inv_ref[...])
    ca, sa = ca[0:1], sa[0:1]
    cosb, sinb = cosb_ref[...], sinb_ref[...]
    cos = ca * cosb - sa * sinb
    sin = sa * cosb + ca * sinb
    sin_signed = jnp.where(first_half, -sin, sin)

    h = _rms(x_ref[0], g_ref[...]).astype(BF16)

    zf = _dot(h, wf_ref[...]) + bf_ref[...]

    zq = _dot(h, wq_ref[...])
    for p in range(4):
        q = _rope_chunk(zq[:, p * LANES:(p + 1) * LANES], cos, sin_signed, first_half)
        qT_ref[0, p] = (q * (QK_SCALE * LOG2E)).T.astype(BF16)
    for p in range(4):
        q = zq[:, GROUP_COLS + p * LANES:GROUP_COLS + (p + 1) * LANES]
        qT_ref[0, 4 + p] = (q * (QK_SCALE * LOG2E)).T.astype(BF16)

    lf = jnp.where(lane < FX_HEADS, _log_sigmoid(zf), 0.0)
    lfT_ref[0] = lf.T[:FX_HEADS, :]
    cs = _tri_cumsum(tri_ref[...], lf) + carry_ref[0:1, :]
    carry_ref[0:1, :] = cs[T - 1:T, :]

    zk = _dot(h, wk_ref[...])
    for p in range(4):
        k = _rope_chunk(zk[:, p * LANES:(p + 1) * LANES], cos, sin_signed, first_half)
        ka_ref[0, pl.ds(p, T, stride=DA_HEADS), :] = k
        katt_ref[0, p] = k.astype(BF16)
    for p in range(4):
        kf = zk[:, GROUP_COLS + p * LANES:GROUP_COLS + (p + 1) * LANES]
        kfT_ref[0, p * LANES:(p + 1) * LANES, :] = kf.T
        katt_ref[0, 4 + p] = kf.astype(BF16)

    cs2 = cs * LOG2E
    k1, k2, k3 = _split3(cs2)
    aux = (_dot(k1, sel_ref[0]) + _dot(k2, sel_ref[1])) + _dot(k3, sel_ref[2])
    lane4 = lax.broadcasted_iota(jnp.int32, aux.shape, 1) % LANES
    aux = jnp.where((lane4 >= AUX_ONE) & (lane4 < AUX_ONE + 3), 1.0, aux)
    for g in range(4):
        kx_ref[0, g] = aux[:, g * LANES:(g + 1) * LANES].astype(BF16)
    q1, q2, q3 = (t.astype(F32) for t in _split3(cs2.T))
    r = lax.broadcasted_iota(jnp.int32, (AUX_ROWS, T), 0)
    for hd in range(FX_HEADS):
        key_rows = (r >= AUX_KEY[hd % 2]) & (r < AUX_KEY[hd % 2] + 3)
        blk = jnp.where(key_rows, -1.0, 0.0)
        for t, qt in enumerate((q1, q2, q3)):
            blk = jnp.where(r == AUX_ONE + t, jnp.broadcast_to(qt[hd:hd + 1], (AUX_ROWS, T)), blk)
        yq_ref[0, hd] = blk.astype(BF16)

    zv = _dot(h, wv_ref[...])
    for p in range(4):
        v = zv[:, p * LANES:(p + 1) * LANES]
        va_ref[0, pl.ds(p, T, stride=DA_HEADS), :] = v
        vT_ref[0, p] = v.T.astype(BF16)
    for p in range(4):
        vT = zv[:, GROUP_COLS + p * LANES:GROUP_COLS + (p + 1) * LANES].T
        vfT_ref[0, p * LANES:(p + 1) * LANES, :] = vT
        vT_ref[0, 4 + p] = vT.astype(BF16)


def _const_spec(shape):
    n = len(shape)
    return pl.BlockSpec(shape, lambda *_: (0,) * n, pipeline_mode=pl.Buffered(1))


def _proj_prompt(x, g, wq, wk, wv, wf, bf, inv, tri, sel):
    B, S, D = x.shape
    T = PROJ_T
    tok = lambda w: pl.BlockSpec((1, T, w), lambda b, s: (b, s, 0))
    rows4 = pl.BlockSpec((1, T * DA_HEADS, LANES), lambda b, s: (b, s, 0))
    featT = lambda n: pl.BlockSpec((1, n, T), lambda b, s: (b, 0, s))
    out_shape = (
        jax.ShapeDtypeStruct((B, S * DA_HEADS, LANES), F32),
        jax.ShapeDtypeStruct((B, S * DA_HEADS, LANES), F32),
        jax.ShapeDtypeStruct((B, GROUP_COLS, S), F32),
        jax.ShapeDtypeStruct((B, GROUP_COLS, S), F32),
        jax.ShapeDtypeStruct((B, FX_HEADS, S), F32),
        jax.ShapeDtypeStruct((B, N_PAIRS, S, LANES), BF16),
        jax.ShapeDtypeStruct((B, N_PAIRS, LANES, S), BF16),
        jax.ShapeDtypeStruct((B, N_PAIRS, LANES, S), BF16),
        jax.ShapeDtypeStruct((B, 4, S, LANES), BF16),
        jax.ShapeDtypeStruct((B, FX_HEADS, AUX_ROWS, S), BF16),
    )
    out_specs = (
        rows4, rows4, featT(GROUP_COLS), featT(GROUP_COLS), featT(FX_HEADS),
        pl.BlockSpec((1, N_PAIRS, T, LANES), lambda b, s: (b, 0, s, 0)),
        pl.BlockSpec((1, N_PAIRS, LANES, T), lambda b, s: (b, 0, 0, s)),
        pl.BlockSpec((1, N_PAIRS, LANES, T), lambda b, s: (b, 0, 0, s)),
        pl.BlockSpec((1, 4, T, LANES), lambda b, s: (b, 0, s, 0)),
        pl.BlockSpec((1, FX_HEADS, AUX_ROWS, T), lambda b, s: (b, 0, 0, s)),
    )
    in_specs = [tok(D), _const_spec(g.shape), _const_spec(wq.shape), _const_spec(wk.shape),
                _const_spec(wv.shape), _const_spec(wf.shape), _const_spec(bf.shape),
                _const_spec(inv.shape), _const_spec(tri.shape), _const_spec(sel.shape)]
    return pl.pallas_call(
        functools.partial(_proj_prompt_kernel, T=T),
        grid=(B, S // T),
        in_specs=in_specs,
        out_specs=out_specs,
        out_shape=out_shape,
        scratch_shapes=[pltpu.VMEM((8, LANES), F32), pltpu.VMEM((T, LANES), F32), pltpu.VMEM((T, LANES), F32)],
        compiler_params=pltpu.CompilerParams(dimension_semantics=("arbitrary", "arbitrary"),
                                             vmem_limit_bytes=VMEM_LIMIT),
        name="proj_prompt",
    )(x, g, wq, wk, wv, wf, bf, inv, tri, sel)


def _lambda_value(lamv_ref):
    v = lamv_ref[...]
    d1 = jnp.sum(v[0:1] * v[1:2], axis=1, keepdims=True)
    d2 = jnp.sum(v[2:3] * v[3:4], axis=1, keepdims=True)
    return jnp.exp(d1) - jnp.exp(d2) + LAM_INIT


def _flash_step(qi, ki, qT_ref, k_ref, kx_ref, vT_ref, yq_ref, lamv_ref, gcol_ref, o_ref, m_ref, acc_ref,
                s_buf, T):
    @pl.when(ki == 0)
    def _():
        m_ref[...] = jnp.full_like(m_ref, NEG_BIG)
        acc_ref[...] = jnp.zeros_like(acc_ref)

    def step(masked):
        H = T // 2
        if masked:
            key_i = lax.broadcasted_iota(jnp.int32, (H, T), 0)
            qry_i = lax.broadcasted_iota(jnp.int32, (H, T), 1)
            valid_top = key_i <= qry_i
            valid_bot = valid_top[:, :H]
        zeros = jnp.zeros((64, T), BF16)
        aux_pad = jnp.zeros((LANES - AUX_ROWS, T), BF16)

        n_slots = s_buf.shape[0]

        def scores(i):
            pair, w = i // 2, i % 2
            slot = i % n_slots
            qTp = qT_ref[0, pair]
            wq = jnp.concatenate([qTp[:64], zeros] if w == 0 else [zeros, qTp[64:]], axis=0)
            lhs = k_ref[0, pair]
            if pair >= DA_HEADS:
                lhs = jnp.concatenate([lhs, kx_ref[0, pair - DA_HEADS]], axis=1)
                wq = jnp.concatenate([wq, yq_ref[0, i - 2 * DA_HEADS], aux_pad], axis=0)
            if not masked:
                s = _dot(lhs, wq)
                s_buf[slot] = s
                return jnp.max(s, axis=0, keepdims=True)
            s_top = jnp.where(valid_top, _dot(lhs[:H], wq), NEG_BIG)
            s_bot = jnp.where(valid_bot, _dot(lhs[H:], wq[:, H:]), NEG_BIG)
            s_buf[slot, 0:H, :] = s_top
            s_buf[slot, H:, H:] = s_bot
            top_max = jnp.max(s_top, axis=0, keepdims=True)
            right_max = jnp.maximum(top_max[:, H:], jnp.max(s_bot, axis=0, keepdims=True))
            return jnp.concatenate([top_max[:, :H], right_max], axis=1)

        def online(i, tile_max):
            slot = i % n_slots
            m_old = m_ref[i]
            m_new = jnp.maximum(m_old, tile_max)
            m_ref[i] = m_new
            alpha = jnp.exp2(m_old - m_new)
            if not masked:
                return alpha, (jnp.exp2(s_buf[slot] - m_new).astype(BF16),)
            p_top = jnp.exp2(s_buf[slot, 0:H, :] - m_new).astype(BF16)
            p_bot = jnp.exp2(s_buf[slot, H:, H:] - m_new[:, H:]).astype(BF16)
            return alpha, (p_top, p_bot)

        def weighted_values(vals, p):
            if not masked:
                return _dot(vals, p[0])
            p_top, p_bot = p
            right = _dot(vals[:, H:], p_bot)
            return _dot(vals[:, :H], p_top) + jnp.concatenate([jnp.zeros_like(right), right], axis=1)

        ones_blk = jnp.where(lax.broadcasted_iota(jnp.int32, (DEN_ROWS, T), 0) == 0, 1.0, 0.0).astype(BF16)

        n_soft = 2 * N_PAIRS
        tile_max = [scores(j) for j in range(QK_AHEAD)]
        for i in range(n_soft):
            if i + QK_AHEAD < n_soft:
                tile_max.append(scores(i + QK_AHEAD))
            alpha, p = online(i, tile_max.pop(0))
            pair = i // 2
            if pair < DA_HEADS:
                vals = vT_ref[0, pair]
            else:
                vals = vT_ref[0, pair, (i % 2) * FX_DIM:(i % 2 + 1) * FX_DIM, :]
            rows = pl.ds(0, vals.shape[0] + DEN_ROWS)
            acc_ref[i, rows, :] = (alpha * acc_ref[i, rows, :]
                                   + weighted_values(jnp.concatenate([vals, ones_blk], axis=0), p))

    @pl.when(ki < qi)
    def _():
        step(False)

    @pl.when(ki == qi)
    def _():
        step(True)

        def normalized(i, n_rows):
            return acc_ref[i, 0:n_rows, :] * (1.0 / acc_ref[i, n_rows:n_rows + 1, :])

        lam = _lambda_value(lamv_ref)
        for h in range(DA_HEADS):
            o = normalized(2 * h, DA_V_DIM) - lam * normalized(2 * h + 1, DA_V_DIM)
            ms = jnp.mean(o * o, axis=0, keepdims=True)
            y = o * lax.rsqrt(ms + NORM_EPS) * gcol_ref[...] * (1.0 - LAM_INIT)
            o_ref[0, :, h * LANES:(h + 1) * LANES] = y.T.astype(o_ref.dtype)
        for g in range(FX_HEADS // 2):
            i0 = 2 * DA_HEADS + 2 * g
            y = jnp.concatenate([normalized(i0, FX_DIM), normalized(i0 + 1, FX_DIM)], axis=0)
            o_ref[0, :, GROUP_COLS + g * LANES:GROUP_COLS + (g + 1) * LANES] = y.T.astype(o_ref.dtype)


def _flash_kernel(qtab_ref, ktab_ref, *refs, T):
    t = pl.program_id(1)
    _flash_step(qtab_ref[t], ktab_ref[t], *refs, T)


def _flash_prompt(qT, katt, kx, vT, yq, lamv, gcol):
    B, _, _, S = qT.shape
    T = ATT_T
    n = S // T
    qtab = jnp.asarray([qi for qi in range(n) for _ in range(qi + 1)], jnp.int32)
    ktab = jnp.asarray([ki for qi in range(n) for ki in range(qi + 1)], jnp.int32)
    q_cols = lambda rows: pl.BlockSpec((1, rows[0], rows[1], T), lambda b, t, qt, kt: (b, 0, 0, qt[t]))
    in_specs = [
        q_cols((N_PAIRS, LANES)),
        pl.BlockSpec((1, N_PAIRS, T, LANES), lambda b, t, qt, kt: (b, 0, kt[t], 0)),
        pl.BlockSpec((1, 4, T, LANES), lambda b, t, qt, kt: (b, 0, kt[t], 0)),
        pl.BlockSpec((1, N_PAIRS, LANES, T), lambda b, t, qt, kt: (b, 0, 0, kt[t])),
        q_cols((FX_HEADS, AUX_ROWS)),
        pl.BlockSpec(lamv.shape, lambda b, t, qt, kt: (0, 0)),
        pl.BlockSpec(gcol.shape, lambda b, t, qt, kt: (0, 0)),
    ]
    return pl.pallas_call(
        functools.partial(_flash_kernel, T=T),
        grid_spec=pltpu.PrefetchScalarGridSpec(
            num_scalar_prefetch=2,
            grid=(B, n * (n + 1) // 2),
            in_specs=in_specs,
            out_specs=pl.BlockSpec((1, T, 2 * GROUP_COLS), lambda b, t, qt, kt: (b, qt[t], 0)),
            scratch_shapes=[
                pltpu.VMEM((2 * N_PAIRS, 1, T), F32),
                pltpu.VMEM((2 * N_PAIRS, DA_V_DIM + DEN_ROWS, T), F32),
                pltpu.VMEM((QK_AHEAD + 1, T, T), F32),
            ],
        ),
        out_shape=jax.ShapeDtypeStruct((B, S, 2 * GROUP_COLS), BF16),
        compiler_params=pltpu.CompilerParams(dimension_semantics=("arbitrary", "arbitrary"),
                                             vmem_limit_bytes=VMEM_LIMIT),
        name="flash_prompt",
    )(qtab, ktab, qT, katt, kx, vT, yq, lamv, gcol)


def _ff_chunks(n_ff):
    assert n_ff % MXU_TILE == 0
    tiles = n_ff // MXU_TILE
    first = (tiles + 1) // 2 * MXU_TILE
    return ((0, first), (first, n_ff - first))


def _gated_chunk(u, g, gm1, gm2, wd_ref, cw_ref, cb_ref, cols):
    cv = cb_ref[:, cols] + cw_ref[0:1, cols] * gm2 + cw_ref[1:2, cols] * gm1 + cw_ref[2:3, cols] * g
    act = jax.nn.gelu(cv, approximate=True) * u
    return _dot(act.astype(BF16), wd_ref[cols, :])


def _post_prompt_kernel(x_ref, o_ref, wo_ref, g1_ref, g2_ref, wg_ref, wu_ref, cw_ref, cb_ref, wd_ref, g3_ref,
                        y_ref, st_ref, gext_ref, *, T):
    s = pl.program_id(1)
    n_ff = wg_ref.shape[1]

    @pl.when(s == 0)
    def _():
        gext_ref[0:8, :] = jnp.zeros((8, n_ff), F32)

    x1 = x_ref[0] + _rms(_dot(o_ref[0], wo_ref[...]), g1_ref[...])
    h = _rms(x1, g2_ref[...]).astype(BF16)
    out = jnp.zeros_like(x1)
    for c0, width in _ff_chunks(n_ff):
        cols = pl.ds(c0, width)
        g = _dot(h, wg_ref[:, cols])
        gext_ref[8:T + 8, cols] = g
        gm1 = gext_ref[7:T + 7, cols]
        gm2 = gext_ref[6:T + 6, cols]
        u = _dot(h, wu_ref[:, cols])
        out = out + _gated_chunk(u, g, gm1, gm2, wd_ref, cw_ref, cb_ref, cols)
    last = gext_ref[T + 6:T + 8, :]
    gext_ref[6:8, :] = last
    y_ref[0] = x1 + _rms(out, g3_ref[...])

    @pl.when(s == pl.num_programs(1) - 1)
    def _():
        st_ref[0] = last


def _post_prompt(x, o, wo, g1, g2, wg, wu, cw, cb, wd, g3):
    B, S, D = x.shape
    T = FFN_T
    n_ff = wg.shape[1]
    tok = lambda w: pl.BlockSpec((1, T, w), lambda b, s: (b, s, 0))
    in_specs = [tok(D), tok(D)] + [_const_spec(a.shape) for a in (wo, g1, g2, wg, wu, cw, cb, wd, g3)]
    return pl.pallas_call(
        functools.partial(_post_prompt_kernel, T=T),
        grid=(B, S // T),
        in_specs=in_specs,
        out_specs=(tok(D), pl.BlockSpec((1, CONV_W - 1, n_ff), lambda b, s: (b, 0, 0))),
        out_shape=(jax.ShapeDtypeStruct((B, S, D), F32), jax.ShapeDtypeStruct((B, CONV_W - 1, n_ff), F32)),
        scratch_shapes=[pltpu.VMEM((T + 8, n_ff), F32)],
        compiler_params=pltpu.CompilerParams(dimension_semantics=("arbitrary", "arbitrary"),
                                             vmem_limit_bytes=VMEM_LIMIT),
        name="post_prompt",
    )(x, o, wo, g1, g2, wg, wu, cw, cb, wd, g3)


def _proj_sample_kernel(x_ref, g_ref, wq_ref, wk_ref, wv_ref, wf_ref, bf_ref, inv_ref, pos_ref,
                        qa_ref, qf_ref, ka_ref, va_ref, kf_ref, vf_ref, lf_ref):
    n = x_ref.shape[0]
    lane = lax.broadcasted_iota(jnp.int32, (n, LANES), 1)
    first_half = (lane % 64) < 32
    cos, sin = _rope_tables(jnp.broadcast_to(pos_ref[...], (n, LANES)), inv_ref[...])
    sin_signed = jnp.where(first_half, -sin, sin)
    h = _rms(x_ref[...], g_ref[...]).astype(BF16)

    zq = _dot(h, wq_ref[...])
    zk = _dot(h, wk_ref[...])
    for p in range(4):
        cols = slice(p * LANES, (p + 1) * LANES)
        qa_ref[:, cols] = _rope_chunk(zq[:, cols], cos, sin_signed, first_half) * QK_SCALE
        ka_ref[:, cols] = _rope_chunk(zk[:, cols], cos, sin_signed, first_half)
    qf_ref[...] = zq[:, GROUP_COLS:] * QK_SCALE
    kf_ref[...] = zk[:, GROUP_COLS:]
    zv = _dot(h, wv_ref[...])
    va_ref[...] = zv[:, :GROUP_COLS]
    vf_ref[...] = zv[:, GROUP_COLS:]
    lf = _log_sigmoid(_dot(h, wf_ref[...]) + bf_ref[...])
    lf_ref[...] = jnp.where(lane < FX_HEADS, lf, 0.0)


def _proj_sample(x, g, wq, wk, wv, wf, bf, inv, pos):
    n = x.shape[0]
    row = lambda w: jax.ShapeDtypeStruct((n, w), F32)
    return pl.pallas_call(
        _proj_sample_kernel,
        out_shape=(row(GROUP_COLS),) * 6 + (row(LANES),),
        compiler_params=pltpu.CompilerParams(vmem_limit_bytes=VMEM_LIMIT),
        name="proj_sample",
    )(x, g, wq, wk, wv, wf, bf, inv, pos)


def _forget_bias_kernel(pt_ref, pool_ref, lfnew_ref, upper_ref, later_ref, out_ref, *, n_pages):
    for j in range(BIAS_SAMPLES_PER_STEP):
        b = pl.program_id(0) * BIAS_SAMPLES_PER_STEP + j
        x = jnp.concatenate([pool_ref[pt_ref[b, i]] for i in range(n_pages)], axis=0)
        a1, a2, a3 = _split3(x)
        within = (_dot(a1, upper_ref[...]) + _dot(a2, upper_ref[...])) + _dot(a3, upper_ref[...])
        page_total = jnp.broadcast_to(within[:, PAGE_SIZE - 1:PAGE_SIZE], x.shape)
        t1, t2, t3 = _split3(page_total)
        suffix = (_dot(later_ref[...], t1) + _dot(later_ref[...], t2)) + _dot(later_ref[...], t3)
        lf_new = jnp.concatenate([lfnew_ref[j]] * n_pages, axis=0)
        out_ref[j] = (suffix + lf_new) - within


def _forget_bias(page_table, cache_lf, lf_new, upper, later):
    n, n_pages = page_table.shape
    rows = n_pages * FX_HEADS
    per = BIAS_SAMPLES_PER_STEP
    assert n % per == 0

    in_specs = [
        _const_spec(cache_lf.shape),
        pl.BlockSpec((per, FX_HEADS, LANES), lambda b, pt: (b, 0, 0)),
        pl.BlockSpec(upper.shape, lambda b, pt: (0, 0)),
        pl.BlockSpec(later.shape, lambda b, pt: (0, 0)),
    ]
    return pl.pallas_call(
        functools.partial(_forget_bias_kernel, n_pages=n_pages),
        grid_spec=pltpu.PrefetchScalarGridSpec(
            num_scalar_prefetch=1,
            grid=(n // per,),
            in_specs=in_specs,
            out_specs=pl.BlockSpec((per, rows, PAGE_SIZE), lambda b, pt: (b, 0, 0)),
        ),
        out_shape=jax.ShapeDtypeStruct((n, rows, PAGE_SIZE), F32),
        compiler_params=pltpu.CompilerParams(dimension_semantics=("arbitrary",), vmem_limit_bytes=VMEM_LIMIT),
        name="forget_bias",
    )(page_table, cache_lf, lf_new, upper, later)


def _decode_step(c, n_chunks, refs, oa_ref, of_ref, m_ref, l_ref, accd_ref, accf_ref, pps):
    kd_refs, vd_refs = refs[0:pps], refs[pps:2 * pps]
    kf_refs, vf_refs = refs[2 * pps:3 * pps], refs[3 * pps:4 * pps]
    bias_ref, qa_ref, qf_ref, kan_ref, van_ref, kfn_ref, vfn_ref, lamv_ref = refs[4 * pps:]
    nt = (((1,), (1,)), ((), ()))

    lane4 = lax.broadcasted_iota(jnp.int32, (DA_HEADS, LANES), 1)
    q4 = qa_ref[0]
    qd = jnp.concatenate([jnp.where(lane4 < DA_QK_DIM, q4, 0.0), jnp.where(lane4 >= DA_QK_DIM, q4, 0.0)], axis=0)
    row = lax.broadcasted_iota(jnp.int32, (FX_HEADS, GROUP_COLS), 0)
    col = lax.broadcasted_iota(jnp.int32, (FX_HEADS, GROUP_COLS), 1)
    own_f = col // FX_DIM == row
    qf = jnp.where(own_f, jnp.broadcast_to(qf_ref[0], (FX_HEADS, GROUP_COLS)), 0.0)
    own_d = col % DA_HEADS == row % DA_HEADS

    @pl.when(c == 0)
    def _():
        m_ref[...] = jnp.full_like(m_ref, NEG_BIG)
        l_ref[...] = jnp.zeros_like(l_ref)
        accd_ref[...] = jnp.zeros_like(accd_ref)
        accf_ref[...] = jnp.zeros_like(accf_ref)

    def online(grp, s):
        m_old = m_ref[grp][:, 0:1]
        m_new = jnp.maximum(m_old, jnp.max(s, axis=1, keepdims=True))
        alpha = jnp.exp(m_old - m_new)
        p = jnp.exp(s - m_new)
        l_new = alpha * l_ref[grp][:, 0:1] + jnp.sum(p, axis=1, keepdims=True)
        m_ref[grp] = jnp.broadcast_to(m_new, (8, LANES))
        l_ref[grp] = jnp.broadcast_to(l_new, (8, LANES))
        return alpha, p.astype(BF16)

    qdb = qd.astype(BF16)
    s = jnp.concatenate(
        [jnp.where(own_d, lax.dot_general(qdb, kd_refs[i][0].astype(BF16), nt, preferred_element_type=F32), NEG_BIG)
         for i in range(pps)], axis=1)
    alpha, p = online(0, s)
    w = GROUP_COLS
    pv = _dot(p[:, 0:w], vd_refs[0][0].astype(BF16))
    for i in range(1, pps):
        pv = pv + _dot(p[:, i * w:(i + 1) * w], vd_refs[i][0].astype(BF16))
    accd_ref[...] = alpha * accd_ref[...] + pv

    qfb = qf.astype(BF16)
    s = jnp.concatenate([_dot(qfb, kf_refs[i][0].astype(BF16)) + bias_ref[0, i] for i in range(pps)], axis=1)
    alpha, p = online(1, s)
    w = PAGE_SIZE
    pv = lax.dot_general(p[:, 0:w], vf_refs[0][0].astype(BF16), nt, preferred_element_type=F32)
    for i in range(1, pps):
        pv = pv + lax.dot_general(p[:, i * w:(i + 1) * w], vf_refs[i][0].astype(BF16), nt,
                                  preferred_element_type=F32)
    accf_ref[...] = alpha * accf_ref[...] + pv

    @pl.when(c == n_chunks - 1)
    def _():
        def finish(grp, q, k_new, v_new, acc):
            s = jnp.sum(q * k_new, axis=1, keepdims=True)
            m_old = m_ref[grp][:, 0:1]
            m_new = jnp.maximum(m_old, s)
            alpha = jnp.exp(m_old - m_new)
            p = jnp.exp(s - m_new)
            l_new = alpha * l_ref[grp][:, 0:1] + p
            return (alpha * acc + p * v_new) * (1.0 / l_new)

        twice = lambda a: jnp.concatenate([a, a], axis=0)
        od = finish(0, qd, twice(kan_ref[0]), twice(van_ref[0]), accd_ref[...])
        oa_ref[0] = od[0:DA_HEADS] - _lambda_value(lamv_ref) * od[DA_HEADS:]
        of = finish(1, qf, kfn_ref[0], vfn_ref[0], accf_ref[...])
        of_ref[0] = jnp.sum(jnp.where(own_f, of, 0.0), axis=0, keepdims=True)


def _decode_kernel(pt_ref, *refs, pps):
    del pt_ref
    n_in = 4 * pps + 8
    _decode_step(pl.program_id(1), pl.num_programs(1), refs[:n_in], *refs[n_in:], pps)


def _decode_attention(page_table, cdk, cdv, cfk, cfv, bias, qa, qf, ka_new, va_new, kf_new, vf_new, lamv):
    n, n_pages = page_table.shape
    pps = PAGES_PER_STEP
    steps = n_pages // pps

    def page_spec(i):
        return pl.BlockSpec((1, GROUP_COLS, PAGE_SIZE), lambda b, c, pt: (pt[b, c * pps + i], 0, 0))

    row_spec = pl.BlockSpec((1, 1, GROUP_COLS), lambda b, c, pt: (b, 0, 0))
    head_spec = pl.BlockSpec((1, DA_HEADS, LANES), lambda b, c, pt: (b, 0, 0))
    in_specs = ([page_spec(i) for i in range(pps)] * 4
                + [pl.BlockSpec((1, pps, FX_HEADS, PAGE_SIZE), lambda b, c, pt: (b, c, 0, 0))]
                + [head_spec, row_spec, head_spec, head_spec, row_spec, row_spec]
                + [pl.BlockSpec(lamv.shape, lambda b, c, pt: (0, 0))])
    caches = [cdk] * pps + [cdv] * pps + [cfk] * pps + [cfv] * pps
    return pl.pallas_call(
        functools.partial(_decode_kernel, pps=pps),
        grid_spec=pltpu.PrefetchScalarGridSpec(
            num_scalar_prefetch=1,
            grid=(n, steps),
            in_specs=in_specs,
            out_specs=(head_spec, row_spec),
            scratch_shapes=[pltpu.VMEM((2, 8, LANES), F32), pltpu.VMEM((2, 8, LANES), F32),
                            pltpu.VMEM((8, LANES), F32), pltpu.VMEM((8, GROUP_COLS), F32)],
        ),
        out_shape=(jax.ShapeDtypeStruct((n, DA_HEADS, LANES), F32), jax.ShapeDtypeStruct((n, 1, GROUP_COLS), F32)),
        compiler_params=pltpu.CompilerParams(dimension_semantics=("arbitrary", "arbitrary"),
                                             vmem_limit_bytes=VMEM_LIMIT),
        name="decode_attention",
    )(page_table, *caches, bias, qa, qf, ka_new, va_new, kf_new, vf_new, lamv)


def _post_sample_kernel(x_ref, oa_ref, of_ref, sg_ref, wo_ref, g1_ref, g2_ref, wg_ref, wu_ref, cw_ref, cb_ref,
                        wd_ref, g3_ref, p0_ref, p1_ref, y_ref, gate_ref):
    n_ff = wg_ref.shape[1]
    heads = [_rms(oa_ref[:, h * LANES:(h + 1) * LANES], sg_ref[...]) * (1.0 - LAM_INIT) for h in range(DA_HEADS)]
    o = jnp.concatenate(heads + [of_ref[...]], axis=1).astype(BF16)
    x1 = x_ref[...] + _rms(_dot(o, wo_ref[...]), g1_ref[...])
    h = _rms(x1, g2_ref[...]).astype(BF16)
    out = jnp.zeros_like(x1)
    for c0, width in _ff_chunks(n_ff):
        cols = pl.ds(c0, width)
        g = _dot(h, wg_ref[:, cols])
        gate_ref[:, cols] = g
        u = _dot(h, wu_ref[:, cols])
        out = out + _gated_chunk(u, g, p1_ref[:, cols], p0_ref[:, cols], wd_ref, cw_ref, cb_ref, cols)
    y_ref[...] = x1 + _rms(out, g3_ref[...])


def _post_sample(x, oa, of, sg, wo, g1, g2, wg, wu, cw, cb, wd, g3, p0, p1):
    n, d = x.shape
    return pl.pallas_call(
        _post_sample_kernel,
        out_shape=(jax.ShapeDtypeStruct((n, d), F32), jax.ShapeDtypeStruct((n, wg.shape[1]), F32)),
        compiler_params=pltpu.CompilerParams(vmem_limit_bytes=VMEM_LIMIT),
        name="post_sample",
    )(x, oa, of, sg, wo, g1, g2, wg, wu, cw, cb, wd, g3, p0, p1)


def kernel(x_prompt, x_sample, cache_diff_k, cache_diff_v, cache_fox_k, cache_fox_v, cache_fox_logf, state_ffn_conv, page_table, attn_pre_g, w_in, b_f, lam_q1, lam_k1, lam_q2, lam_k2, subln_g, w_o, attn_post_g, ffn_pre_g, w_gate, w_up, conv_w, conv_b, w_down, ffn_post_g):
    B, S, D = x_prompt.shape
    DB, DS, _ = x_sample.shape
    assert DS == 1 and w_in.shape[0] == 1, "one layer, one new token per sample"
    n_pool = cache_diff_k.shape[1]
    n_pages = page_table.shape[1]
    past_len = n_pages * PAGE_SIZE
    l = 0

    w = w_in[l]
    sec = lambda i: w[:, i * GROUP_COLS:(i + 1) * GROUP_COLS]
    wq = jnp.concatenate([sec(0), sec(3)], axis=1).astype(BF16)
    wk = jnp.concatenate([sec(1), sec(4)], axis=1).astype(BF16)
    wv = jnp.concatenate([sec(2), sec(5)], axis=1).astype(BF16)
    wf = jnp.pad(w[:, 6 * GROUP_COLS:], ((0, 0), (0, LANES - FX_HEADS))).astype(BF16)
    bf = jnp.pad(b_f[l], (0, LANES - FX_HEADS)).reshape(1, LANES)
    wo, wg, wu, wd = (a[l].astype(BF16) for a in (w_o, w_gate, w_up, w_down))
    row = lambda a: a[l].reshape(1, -1)
    g_pre, g_post, g_ffn, g_out, sg, cb = (row(a) for a in (attn_pre_g, attn_post_g, ffn_pre_g, ffn_post_g, subln_g, conv_b))
    cw = conv_w[l]
    half = DA_QK_DIM // 2
    inv = ROPE_THETA ** (-jnp.arange(half, dtype=F32) / half)
    inv = jnp.tile(inv, LANES // half).reshape(1, LANES)
    tri = jnp.tril(jnp.ones((PROJ_T, PROJ_T), BF16))
    lamv = jnp.stack([lam_q1[l], lam_k1[l], lam_q2[l], lam_k2[l]])

    sel = np.zeros((3, LANES, 4 * LANES), np.float32)
    for hd in range(FX_HEADS):
        for term in range(3):
            sel[term, hd, (hd // 2) * LANES + AUX_KEY[hd % 2] + term] = 1.0
    ka4, va4, kfT, vfT, lfT, katt, qT, vT, kx, yq = _proj_prompt(x_prompt, g_pre, wq, wk, wv, wf, bf, inv, tri,
                                                                 jnp.asarray(sel, BF16))
    heads_last = lambda t: jnp.transpose(t.reshape(B, FX_HEADS, FX_DIM, S), (0, 3, 1, 2))
    kf, vf, logf = heads_last(kfT), heads_last(vfT), jnp.transpose(lfT, (0, 2, 1))

    pos = jnp.full((1, LANES), past_len, F32)
    qa_s, qf_s, ka_s, va_s, kf_s, vf_s, lf_s = _proj_sample(x_sample[:, 0, :], g_pre, wq, wk, wv, wf, bf, inv, pos)
    rows_d = lambda c: c[l].reshape(n_pool, PAGE_SIZE * DA_HEADS, 2 * DA_QK_DIM)
    rows_f = lambda c: jnp.transpose(c[l], (0, 2, 3, 1)).reshape(n_pool, GROUP_COLS, PAGE_SIZE)
    lf_pages = jnp.transpose(cache_fox_logf[l], (0, 2, 1))
    key_i = jnp.arange(PAGE_SIZE)
    upper = (key_i[:, None] <= key_i[None, :]).astype(BF16)
    row_i = jnp.arange(n_pages * FX_HEADS)
    later = ((row_i[:, None] % FX_HEADS == row_i[None, :] % FX_HEADS)
             & (row_i[None, :] // FX_HEADS >= row_i[:, None] // FX_HEADS)).astype(BF16)
    lf_new = jnp.broadcast_to(lf_s[:, :FX_HEADS, None], (DB, FX_HEADS, LANES))
    bias = _forget_bias(page_table, lf_pages, lf_new, upper, later).reshape(DB, n_pages, FX_HEADS, PAGE_SIZE)
    r3 = lambda a: a.reshape(DB, 1, GROUP_COLS)
    r4 = lambda a: a.reshape(DB, DA_HEADS, LANES)
    oa_s, of_s = _decode_attention(page_table, rows_d(cache_diff_k), rows_d(cache_diff_v), rows_f(cache_fox_k),
                                   rows_f(cache_fox_v), bias, r4(qa_s), r3(qf_s), r4(ka_s), r4(va_s), r3(kf_s),
                                   r3(vf_s), lamv)
    o = _flash_prompt(qT, katt, kx, vT, yq, lamv, sg.reshape(DA_V_DIM, 1))
    y_prompt, conv_p = _post_prompt(x_prompt, o, wo, g_post, g_ffn, wg, wu, cw, cb, wd, g_out)
    prev = state_ffn_conv[l]
    y_sample, gate_s = _post_sample(x_sample[:, 0, :], oa_s.reshape(DB, GROUP_COLS), of_s[:, 0, :], sg, wo, g_post,
                                    g_ffn, wg, wu, cw, cb, wd, g_out, prev[:, 0, :], prev[:, 1, :])
    conv_s = jnp.stack([prev[:, 1, :], gate_s], axis=1)

    lead = lambda a, shape: a.reshape((1,) + shape)
    return (
        y_prompt,
        y_sample.reshape(DB, DS, D),
        lead(ka4, (B, S, DA_HEADS, 2 * DA_QK_DIM)),
        lead(va4, (B, S, DA_HEADS, DA_V_DIM)),
        lead(kf, (B, S, FX_HEADS, FX_DIM)),
        lead(vf, (B, S, FX_HEADS, FX_DIM)),
        lead(logf, (B, S, FX_HEADS)),
        lead(conv_p, (B, CONV_W - 1, conv_p.shape[-1])),
        lead(ka_s, (DB, DS, DA_HEADS, 2 * DA_QK_DIM)),
        lead(va_s, (DB, DS, DA_HEADS, DA_V_DIM)),
        lead(kf_s, (DB, DS, FX_HEADS, FX_DIM)),
        lead(vf_s, (DB, DS, FX_HEADS, FX_DIM)),
        lead(lf_s[:, :FX_HEADS], (DB, DS, FX_HEADS)),
        lead(conv_s, (DB, CONV_W - 1, conv_s.shape[-1])),
    )
```

```python
import functools
import math

import jax
import jax.numpy as jnp
import numpy as np
from jax import lax
from jax.experimental import pallas as pl
from jax.experimental.pallas import tpu as pltpu

F32 = jnp.float32
BF16 = jnp.bfloat16

DA_HEADS = 4
DA_QK_DIM = 64
DA_V_DIM = 128
FX_HEADS = 8
FX_DIM = 64
GROUP_COLS = 512
CONV_W = 3
ROPE_THETA = 10000.0
NORM_EPS = 1e-6
PAGE_SIZE = 128
LAM_INIT = 0.8 - 0.6 * math.exp(-0.3 * 0)
QK_SCALE = 0.125

LANES = 128
MXU_TILE = 256
N_PAIRS = 8
NEG_BIG = -1e30
LOG2E = 1.4426950408889634
AUX_ROWS = 16
AUX_KEY = (0, 3)
AUX_ONE = 6
DEN_ROWS = 16

PROJ_T = 512
ATT_T = 512
FFN_T = 512
QK_AHEAD = 2
PAGES_PER_STEP = 16
BIAS_SAMPLES_PER_STEP = 4
VMEM_LIMIT = 56 * 1024 * 1024


def _rms(x, g):
    return x * lax.rsqrt(jnp.mean(x * x, axis=-1, keepdims=True) + NORM_EPS) * g


def _dot(a, b):
    return jnp.dot(a, b, preferred_element_type=F32)


def _split3(x):
    a1 = x.astype(BF16)
    r1 = x - a1.astype(F32)
    a2 = r1.astype(BF16)
    a3 = (r1 - a2.astype(F32)).astype(BF16)
    return a1, a2, a3


def _tri_cumsum(tri, x):
    a1, a2, a3 = _split3(x)
    return (_dot(tri, a1) + _dot(tri, a2)) + _dot(tri, a3)


def _log_sigmoid(z):
    return jnp.minimum(z, 0.0) - jnp.log1p(jnp.exp(-jnp.abs(z)))


def _rope_tables(pos, inv):
    ang = pos * inv
    return jnp.cos(ang), jnp.sin(ang)


def _rope_chunk(z, cos, sin_signed, first_half):
    swapped = jnp.where(first_half, pltpu.roll(z, LANES - 32, 1), pltpu.roll(z, 32, 1))
    return z * cos + swapped * sin_signed


def _proj_prompt_kernel(x_ref, g_ref, wq_ref, wk_ref, wv_ref, wf_ref, bf_ref, inv_ref, tri_ref, sel_ref,
                        ka_ref, va_ref, kfT_ref, vfT_ref, lfT_ref, katt_ref, qT_ref, vT_ref, kx_ref, yq_ref,
                        carry_ref, cosb_ref, sinb_ref, *, T):
    b = pl.program_id(0)
    s = pl.program_id(1)
    lane = lax.broadcasted_iota(jnp.int32, (T, LANES), 1)
    first_half = (lane % 64) < 32

    @pl.when((b == 0) & (s == 0))
    def _():
        row = lax.broadcasted_iota(jnp.int32, (T, LANES), 0).astype(F32)
        cosb, sinb = _rope_tables(row, inv_ref[...])
        cosb_ref[...] = cosb
        sinb_ref[...] = sinb

    @pl.when(s == 0)
    def _():
        carry_ref[...] = jnp.zeros_like(carry_ref)

    base = jnp.full((8, LANES), s * T, jnp.int32).astype(F32)
    ca, sa = _rope_tables(base, inv_ref[...])
    ca, sa = ca[0:1], sa[0:1]
    cosb, sinb = cosb_ref[...], sinb_ref[...]
    cos = ca * cosb - sa * sinb
    sin = sa * cosb + ca * sinb
    sin_signed = jnp.where(first_half, -sin, sin)

    h = _rms(x_ref[0], g_ref[...]).astype(BF16)

    zf = _dot(h, wf_ref[...]) + bf_ref[...]

    zq = _dot(h, wq_ref[...])
    for p in range(4):
        q = _rope_chunk(zq[:, p * LANES:(p + 1) * LANES], cos, sin_signed, first_half)
        qT_ref[0, p] = (q * (QK_SCALE * LOG2E)).T.astype(BF16)
    for p in range(4):
        q = zq[:, GROUP_COLS + p * LANES:GROUP_COLS + (p + 1) * LANES]
        qT_ref[0, 4 + p] = (q * (QK_SCALE * LOG2E)).T.astype(BF16)

    lf = jnp.where(lane < FX_HEADS, _log_sigmoid(zf), 0.0)
    lfT_ref[0] = lf.T[:FX_HEADS, :]
    cs = _tri_cumsum(tri_ref[...], lf) + carry_ref[0:1, :]
    carry_ref[0:1, :] = cs[T - 1:T, :]

    zk = _dot(h, wk_ref[...])
    for p in range(4):
        k = _rope_chunk(zk[:, p * LANES:(p + 1) * LANES], cos, sin_signed, first_half)
        ka_ref[0, pl.ds(p, T, stride=DA_HEADS), :] = k
        katt_ref[0, p] = k.astype(BF16)
    for p in range(4):
        kf = zk[:, GROUP_COLS + p * LANES:GROUP_COLS + (p + 1) * LANES]
        kfT_ref[0, p * LANES:(p + 1) * LANES, :] = kf.T
        katt_ref[0, 4 + p] = kf.astype(BF16)

    cs2 = cs * LOG2E
    k1, k2, k3 = _split3(cs2)
    aux = (_dot(k1, sel_ref[0]) + _dot(k2, sel_ref[1])) + _dot(k3, sel_ref[2])
    lane4 = lax.broadcasted_iota(jnp.int32, aux.shape, 1) % LANES
    aux = jnp.where((lane4 >= AUX_ONE) & (lane4 < AUX_ONE + 3), 1.0, aux)
    for g in range(4):
        kx_ref[0, g] = aux[:, g * LANES:(g + 1) * LANES].astype(BF16)
    q1, q2, q3 = (t.astype(F32) for t in _split3(cs2.T))
    r = lax.broadcasted_iota(jnp.int32, (AUX_ROWS, T), 0)
    for hd in range(FX_HEADS):
        key_rows = (r >= AUX_KEY[hd % 2]) & (r < AUX_KEY[hd % 2] + 3)
        blk = jnp.where(key_rows, -1.0, 0.0)
        for t, qt in enumerate((q1, q2, q3)):
            blk = jnp.where(r == AUX_ONE + t, jnp.broadcast_to(qt[hd:hd + 1], (AUX_ROWS, T)), blk)
        yq_ref[0, hd] = blk.astype(BF16)

    zv = _dot(h, wv_ref[...])
    for p in range(4):
        v = zv[:, p * LANES:(p + 1) * LANES]
        va_ref[0, pl.ds(p, T, stride=DA_HEADS), :] = v
        vT_ref[0, p] = v.T.astype(BF16)
    for p in range(4):
        vT = zv[:, GROUP_COLS + p * LANES:GROUP_COLS + (p + 1) * LANES].T
        vfT_ref[0, p * LANES:(p + 1) * LANES, :] = vT
        vT_ref[0, 4 + p] = vT.astype(BF16)


def _const_spec(shape):
    n = len(shape)
    return pl.BlockSpec(shape, lambda *_: (0,) * n, pipeline_mode=pl.Buffered(1))


def _proj_prompt(x, g, wq, wk, wv, wf, bf, inv, tri, sel):
    B, S, D = x.shape
    T = PROJ_T
    tok = lambda w: pl.BlockSpec((1, T, w), lambda b, s: (b, s, 0))
    rows4 = pl.BlockSpec((1, T * DA_HEADS, LANES), lambda b, s: (b, s, 0))
    featT = lambda n: pl.BlockSpec((1, n, T), lambda b, s: (b, 0, s))
    out_shape = (
        jax.ShapeDtypeStruct((B, S * DA_HEADS, LANES), F32),
        jax.ShapeDtypeStruct((B, S * DA_HEADS, LANES), F32),
        jax.ShapeDtypeStruct((B, GROUP_COLS, S), F32),
        jax.ShapeDtypeStruct((B, GROUP_COLS, S), F32),
        jax.ShapeDtypeStruct((B, FX_HEADS, S), F32),
        jax.ShapeDtypeStruct((B, N_PAIRS, S, LANES), BF16),
        jax.ShapeDtypeStruct((B, N_PAIRS, LANES, S), BF16),
        jax.ShapeDtypeStruct((B, N_PAIRS, LANES, S), BF16),
        jax.ShapeDtypeStruct((B, 4, S, LANES), BF16),
        jax.ShapeDtypeStruct((B, FX_HEADS, AUX_ROWS, S), BF16),
    )
    out_specs = (
        rows4, rows4, featT(GROUP_COLS), featT(GROUP_COLS), featT(FX_HEADS),
        pl.BlockSpec((1, N_PAIRS, T, LANES), lambda b, s: (b, 0, s, 0)),
        pl.BlockSpec((1, N_PAIRS, LANES, T), lambda b, s: (b, 0, 0, s)),
        pl.BlockSpec((1, N_PAIRS, LANES, T), lambda b, s: (b, 0, 0, s)),
        pl.BlockSpec((1, 4, T, LANES), lambda b, s: (b, 0, s, 0)),
        pl.BlockSpec((1, FX_HEADS, AUX_ROWS, T), lambda b, s: (b, 0, 0, s)),
    )
    in_specs = [tok(D), _const_spec(g.shape), _const_spec(wq.shape), _const_spec(wk.shape),
                _const_spec(wv.shape), _const_spec(wf.shape), _const_spec(bf.shape),
                _const_spec(inv.shape), _const_spec(tri.shape), _const_spec(sel.shape)]
    return pl.pallas_call(
        functools.partial(_proj_prompt_kernel, T=T),
        grid=(B, S // T),
        in_specs=in_specs,
        out_specs=out_specs,
        out_shape=out_shape,
        scratch_shapes=[pltpu.VMEM((8, LANES), F32), pltpu.VMEM((T, LANES), F32), pltpu.VMEM((T, LANES), F32)],
        compiler_params=pltpu.CompilerParams(dimension_semantics=("arbitrary", "arbitrary"),
                                             vmem_limit_bytes=VMEM_LIMIT),
        name="proj_prompt",
    )(x, g, wq, wk, wv, wf, bf, inv, tri, sel)


def _lambda_value(lamv_ref):
    v = lamv_ref[...]
    d1 = jnp.sum(v[0:1] * v[1:2], axis=1, keepdims=True)
    d2 = jnp.sum(v[2:3] * v[3:4], axis=1, keepdims=True)
    return jnp.exp(d1) - jnp.exp(d2) + LAM_INIT


def _flash_step(qi, ki, qT_ref, k_ref, kx_ref, vT_ref, yq_ref, lamv_ref, gcol_ref, o_ref, m_ref, acc_ref,
                s_buf, T):
    @pl.when(ki == 0)
    def _():
        m_ref[...] = jnp.full_like(m_ref, NEG_BIG)
        acc_ref[...] = jnp.zeros_like(acc_ref)

    def step(masked):
        H = T // 2
        if masked:
            key_i = lax.broadcasted_iota(jnp.int32, (H, T), 0)
            qry_i = lax.broadcasted_iota(jnp.int32, (H, T), 1)
            valid_top = key_i <= qry_i
            valid_bot = valid_top[:, :H]
        zeros = jnp.zeros((64, T), BF16)
        aux_pad = jnp.zeros((LANES - AUX_ROWS, T), BF16)

        n_slots = s_buf.shape[0]

        def scores(i):
            pair, w = i // 2, i % 2
            slot = i % n_slots
            qTp = qT_ref[0, pair]
            wq = jnp.concatenate([qTp[:64], zeros] if w == 0 else [zeros, qTp[64:]], axis=0)
            lhs = k_ref[0, pair]
            if pair >= DA_HEADS:
                lhs = jnp.concatenate([lhs, kx_ref[0, pair - DA_HEADS]], axis=1)
                wq = jnp.concatenate([wq, yq_ref[0, i - 2 * DA_HEADS], aux_pad], axis=0)
            if not masked:
                s = _dot(lhs, wq)
                s_buf[slot] = s
                return jnp.max(s, axis=0, keepdims=True)
            s_top = jnp.where(valid_top, _dot(lhs[:H], wq), NEG_BIG)
            s_bot = jnp.where(valid_bot, _dot(lhs[H:], wq[:, H:]), NEG_BIG)
            s_buf[slot, 0:H, :] = s_top
            s_buf[slot, H:, H:] = s_bot
            top_max = jnp.max(s_top, axis=0, keepdims=True)
            right_max = jnp.maximum(top_max[:, H:], jnp.max(s_bot, axis=0, keepdims=True))
            return jnp.concatenate([top_max[:, :H], right_max], axis=1)

        def online(i, tile_max):
            slot = i % n_slots
            m_old = m_ref[i]
            m_new = jnp.maximum(m_old, tile_max)
            m_ref[i] = m_new
            alpha = jnp.exp2(m_old - m_new)
            if not masked:
                return alpha, (jnp.exp2(s_buf[slot] - m_new).astype(BF16),)
            p_top = jnp.exp2(s_buf[slot, 0:H, :] - m_new).astype(BF16)
            p_bot = jnp.exp2(s_buf[slot, H:, H:] - m_new[:, H:]).astype(BF16)
            return alpha, (p_top, p_bot)

        def weighted_values(vals, p):
            if not masked:
                return _dot(vals, p[0])
            p_top, p_bot = p
            right = _dot(vals[:, H:], p_bot)
            return _dot(vals[:, :H], p_top) + jnp.concatenate([jnp.zeros_like(right), right], axis=1)

        ones_blk = jnp.where(lax.broadcasted_iota(jnp.int32, (DEN_ROWS, T), 0) == 0, 1.0, 0.0).astype(BF16)

        n_soft = 2 * N_PAIRS
        tile_max = [scores(j) for j in range(QK_AHEAD)]
        for i in range(n_soft):
            if i + QK_AHEAD < n_soft:
                tile_max.append(scores(i + QK_AHEAD))
            alpha, p = online(i, tile_max.pop(0))
            pair = i // 2
            if pair < DA_HEADS:
                vals = vT_ref[0, pair]
            else:
                vals = vT_ref[0, pair, (i % 2) * FX_DIM:(i % 2 + 1) * FX_DIM, :]
            rows = pl.ds(0, vals.shape[0] + DEN_ROWS)
            acc_ref[i, rows, :] = (alpha * acc_ref[i, rows, :]
                                   + weighted_values(jnp.concatenate([vals, ones_blk], axis=0), p))

    @pl.when(ki < qi)
    def _():
        step(False)

    @pl.when(ki == qi)
    def _():
        step(True)

        def normalized(i, n_rows):
            return acc_ref[i, 0:n_rows, :] * (1.0 / acc_ref[i, n_rows:n_rows + 1, :])

        lam = _lambda_value(lamv_ref)
        for h in range(DA_HEADS):
            o = normalized(2 * h, DA_V_DIM) - lam * normalized(2 * h + 1, DA_V_DIM)
            ms = jnp.mean(o * o, axis=0, keepdims=True)
            y = o * lax.rsqrt(ms + NORM_EPS) * gcol_ref[...] * (1.0 - LAM_INIT)
            o_ref[0, :, h * LANES:(h + 1) * LANES] = y.T.astype(o_ref.dtype)
        for g in range(FX_HEADS // 2):
            i0 = 2 * DA_HEADS + 2 * g
            y = jnp.concatenate([normalized(i0, FX_DIM), normalized(i0 + 1, FX_DIM)], axis=0)
            o_ref[0, :, GROUP_COLS + g * LANES:GROUP_COLS + (g + 1) * LANES] = y.T.astype(o_ref.dtype)


def _flash_kernel(qtab_ref, ktab_ref, *refs, T):
    t = pl.program_id(1)
    _flash_step(qtab_ref[t], ktab_ref[t], *refs, T)


def _flash_prompt(qT, katt, kx, vT, yq, lamv, gcol):
    B, _, _, S = qT.shape
    T = ATT_T
    n = S // T
    qtab = jnp.asarray([qi for qi in range(n) for _ in range(qi + 1)], jnp.int32)
    ktab = jnp.asarray([ki for qi in range(n) for ki in range(qi + 1)], jnp.int32)
    q_cols = lambda rows: pl.BlockSpec((1, rows[0], rows[1], T), lambda b, t, qt, kt: (b, 0, 0, qt[t]))
    in_specs = [
        q_cols((N_PAIRS, LANES)),
        pl.BlockSpec((1, N_PAIRS, T, LANES), lambda b, t, qt, kt: (b, 0, kt[t], 0)),
        pl.BlockSpec((1, 4, T, LANES), lambda b, t, qt, kt: (b, 0, kt[t], 0)),
        pl.BlockSpec((1, N_PAIRS, LANES, T), lambda b, t, qt, kt: (b, 0, 0, kt[t])),
        q_cols((FX_HEADS, AUX_ROWS)),
        pl.BlockSpec(lamv.shape, lambda b, t, qt, kt: (0, 0)),
        pl.BlockSpec(gcol.shape, lambda b, t, qt, kt: (0, 0)),
    ]
    return pl.pallas_call(
        functools.partial(_flash_kernel, T=T),
        grid_spec=pltpu.PrefetchScalarGridSpec(
            num_scalar_prefetch=2,
            grid=(B, n * (n + 1) // 2),
            in_specs=in_specs,
            out_specs=pl.BlockSpec((1, T, 2 * GROUP_COLS), lambda b, t, qt, kt: (b, qt[t], 0)),
            scratch_shapes=[
                pltpu.VMEM((2 * N_PAIRS, 1, T), F32),
                pltpu.VMEM((2 * N_PAIRS, DA_V_DIM + DEN_ROWS, T), F32),
                pltpu.VMEM((QK_AHEAD + 1, T, T), F32),
            ],
        ),
        out_shape=jax.ShapeDtypeStruct((B, S, 2 * GROUP_COLS), BF16),
        compiler_params=pltpu.CompilerParams(dimension_semantics=("arbitrary", "arbitrary"),
                                             vmem_limit_bytes=VMEM_LIMIT),
        name="flash_prompt",
    )(qtab, ktab, qT, katt, kx, vT, yq, lamv, gcol)


def _ff_chunks(n_ff):
    assert n_ff % MXU_TILE == 0
    tiles = n_ff // MXU_TILE
    first = (tiles + 1) // 2 * MXU_TILE
    return ((0, first), (first, n_ff - first))


def _gated_chunk(u, g, gm1, gm2, wd_ref, cw_ref, cb_ref, cols):
    cv = cb_ref[:, cols] + cw_ref[0:1, cols] * gm2 + cw_ref[1:2, cols] * gm1 + cw_ref[2:3, cols] * g
    act = jax.nn.gelu(cv, approximate=True) * u
    return _dot(act.astype(BF16), wd_ref[cols, :])


def _post_prompt_kernel(x_ref, o_ref, wo_ref, g1_ref, g2_ref, wg_ref, wu_ref, cw_ref, cb_ref, wd_ref, g3_ref,
                        y_ref, st_ref, gext_ref, *, T):
    s = pl.program_id(1)
    n_ff = wg_ref.shape[1]

    @pl.when(s == 0)
    def _():
        gext_ref[0:8, :] = jnp.zeros((8, n_ff), F32)

    H = T // 2
    halves = (0, H)
    attn = [_dot(o_ref[0, r0:r0 + H, :], wo_ref[...]) for r0 in halves]
    x1 = [x_ref[0, r0:r0 + H, :] + _rms(a, g1_ref[...]) for r0, a in zip(halves, attn)]
    h = [_rms(v, g2_ref[...]).astype(BF16) for v in x1]
    out = [jnp.zeros_like(v) for v in x1]
    for c0, width in _ff_chunks(n_ff):
        cols = pl.ds(c0, width)
        g = [_dot(hh, wg_ref[:, cols]) for hh in h]
        u = [_dot(hh, wu_ref[:, cols]) for hh in h]
        for r0, gg in zip(halves, g):
            gext_ref[8 + r0:8 + r0 + H, cols] = gg
        for j, r0 in enumerate(halves):
            gm1 = gext_ref[7 + r0:7 + r0 + H, cols]
            gm2 = gext_ref[6 + r0:6 + r0 + H, cols]
            out[j] = out[j] + _gated_chunk(u[j], g[j], gm1, gm2, wd_ref, cw_ref, cb_ref, cols)
    last = gext_ref[T + 6:T + 8, :]
    gext_ref[6:8, :] = last
    for j, r0 in enumerate(halves):
        y_ref[0, r0:r0 + H, :] = x1[j] + _rms(out[j], g3_ref[...])

    @pl.when(s == pl.num_programs(1) - 1)
    def _():
        st_ref[0] = last


def _post_prompt(x, o, wo, g1, g2, wg, wu, cw, cb, wd, g3):
    B, S, D = x.shape
    T = FFN_T
    n_ff = wg.shape[1]
    tok = lambda w: pl.BlockSpec((1, T, w), lambda b, s: (b, s, 0))
    in_specs = [tok(D), tok(D)] + [_const_spec(a.shape) for a in (wo, g1, g2, wg, wu, cw, cb, wd, g3)]
    return pl.pallas_call(
        functools.partial(_post_prompt_kernel, T=T),
        grid=(B, S // T),
        in_specs=in_specs,
        out_specs=(tok(D), pl.BlockSpec((1, CONV_W - 1, n_ff), lambda b, s: (b, 0, 0))),
        out_shape=(jax.ShapeDtypeStruct((B, S, D), F32), jax.ShapeDtypeStruct((B, CONV_W - 1, n_ff), F32)),
        scratch_shapes=[pltpu.VMEM((T + 8, n_ff), F32)],
        compiler_params=pltpu.CompilerParams(dimension_semantics=("arbitrary", "arbitrary"),
                                             vmem_limit_bytes=VMEM_LIMIT),
        name="post_prompt",
    )(x, o, wo, g1, g2, wg, wu, cw, cb, wd, g3)


def _proj_sample_kernel(x_ref, g_ref, wq_ref, wk_ref, wv_ref, wf_ref, bf_ref, inv_ref, pos_ref,
                        qa_ref, qf_ref, ka_ref, va_ref, kf_ref, vf_ref, lf_ref):
    n = x_ref.shape[0]
    lane = lax.broadcasted_iota(jnp.int32, (n, LANES), 1)
    first_half = (lane % 64) < 32
    cos, sin = _rope_tables(jnp.broadcast_to(pos_ref[...], (n, LANES)), inv_ref[...])
    sin_signed = jnp.where(first_half, -sin, sin)
    h = _rms(x_ref[...], g_ref[...]).astype(BF16)

    zq = _dot(h, wq_ref[...])
    zk = _dot(h, wk_ref[...])
    for p in range(4):
        cols = slice(p * LANES, (p + 1) * LANES)
        qa_ref[:, cols] = _rope_chunk(zq[:, cols], cos, sin_signed, first_half) * QK_SCALE
        ka_ref[:, cols] = _rope_chunk(zk[:, cols], cos, sin_signed, first_half)
    qf_ref[...] = zq[:, GROUP_COLS:] * QK_SCALE
    kf_ref[...] = zk[:, GROUP_COLS:]
    zv = _dot(h, wv_ref[...])
    va_ref[...] = zv[:, :GROUP_COLS]
    vf_ref[...] = zv[:, GROUP_COLS:]
    lf = _log_sigmoid(_dot(h, wf_ref[...]) + bf_ref[...])
    lf_ref[...] = jnp.where(lane < FX_HEADS, lf, 0.0)


def _proj_sample(x, g, wq, wk, wv, wf, bf, inv, pos):
    n = x.shape[0]
    row = lambda w: jax.ShapeDtypeStruct((n, w), F32)
    return pl.pallas_call(
        _proj_sample_kernel,
        out_shape=(row(GROUP_COLS),) * 6 + (row(LANES),),
        compiler_params=pltpu.CompilerParams(vmem_limit_bytes=VMEM_LIMIT),
        name="proj_sample",
    )(x, g, wq, wk, wv, wf, bf, inv, pos)


def _forget_bias_kernel(pt_ref, pool_ref, lfnew_ref, upper_ref, later_ref, out_ref, *, n_pages):
    for j in range(BIAS_SAMPLES_PER_STEP):
        b = pl.program_id(0) * BIAS_SAMPLES_PER_STEP + j
        x = jnp.concatenate([pool_ref[pt_ref[b, i]] for i in range(n_pages)], axis=0)
        a1, a2, a3 = _split3(x)
        within = (_dot(a1, upper_ref[...]) + _dot(a2, upper_ref[...])) + _dot(a3, upper_ref[...])
        page_total = jnp.broadcast_to(within[:, PAGE_SIZE - 1:PAGE_SIZE], x.shape)
        t1, t2, t3 = _split3(page_total)
        suffix = (_dot(later_ref[...], t1) + _dot(later_ref[...], t2)) + _dot(later_ref[...], t3)
        lf_new = jnp.concatenate([lfnew_ref[j]] * n_pages, axis=0)
        out_ref[j] = (suffix + lf_new) - within


def _forget_bias(page_table, cache_lf, lf_new, upper, later):
    n, n_pages = page_table.shape
    rows = n_pages * FX_HEADS
    per = BIAS_SAMPLES_PER_STEP
    assert n % per == 0

    in_specs = [
        _const_spec(cache_lf.shape),
        pl.BlockSpec((per, FX_HEADS, LANES), lambda b, pt: (b, 0, 0)),
        pl.BlockSpec(upper.shape, lambda b, pt: (0, 0)),
        pl.BlockSpec(later.shape, lambda b, pt: (0, 0)),
    ]
    return pl.pallas_call(
        functools.partial(_forget_bias_kernel, n_pages=n_pages),
        grid_spec=pltpu.PrefetchScalarGridSpec(
            num_scalar_prefetch=1,
            grid=(n // per,),
            in_specs=in_specs,
            out_specs=pl.BlockSpec((per, rows, PAGE_SIZE), lambda b, pt: (b, 0, 0)),
        ),
        out_shape=jax.ShapeDtypeStruct((n, rows, PAGE_SIZE), F32),
        compiler_params=pltpu.CompilerParams(dimension_semantics=("arbitrary",), vmem_limit_bytes=VMEM_LIMIT),
        name="forget_bias",
    )(page_table, cache_lf, lf_new, upper, later)


def _decode_step(c, n_chunks, refs, oa_ref, of_ref, m_ref, l_ref, accd_ref, accf_ref, pps):
    kd_refs, vd_refs = refs[0:pps], refs[pps:2 * pps]
    kf_refs, vf_refs = refs[2 * pps:3 * pps], refs[3 * pps:4 * pps]
    bias_ref, qa_ref, qf_ref, kan_ref, van_ref, kfn_ref, vfn_ref, lamv_ref = refs[4 * pps:]
    nt = (((1,), (1,)), ((), ()))

    lane4 = lax.broadcasted_iota(jnp.int32, (DA_HEADS, LANES), 1)
    q4 = qa_ref[0]
    qd = jnp.concatenate([jnp.where(lane4 < DA_QK_DIM, q4, 0.0), jnp.where(lane4 >= DA_QK_DIM, q4, 0.0)], axis=0)
    row = lax.broadcasted_iota(jnp.int32, (FX_HEADS, GROUP_COLS), 0)
    col = lax.broadcasted_iota(jnp.int32, (FX_HEADS, GROUP_COLS), 1)
    own_f = col // FX_DIM == row
    qf = jnp.where(own_f, jnp.broadcast_to(qf_ref[0], (FX_HEADS, GROUP_COLS)), 0.0)
    own_d = col % DA_HEADS == row % DA_HEADS

    @pl.when(c == 0)
    def _():
        m_ref[...] = jnp.full_like(m_ref, NEG_BIG)
        l_ref[...] = jnp.zeros_like(l_ref)
        accd_ref[...] = jnp.zeros_like(accd_ref)
        accf_ref[...] = jnp.zeros_like(accf_ref)

    def online(grp, s):
        m_old = m_ref[grp][:, 0:1]
        m_new = jnp.maximum(m_old, jnp.max(s, axis=1, keepdims=True))
        alpha = jnp.exp(m_old - m_new)
        p = jnp.exp(s - m_new)
        l_new = alpha * l_ref[grp][:, 0:1] + jnp.sum(p, axis=1, keepdims=True)
        m_ref[grp] = jnp.broadcast_to(m_new, (8, LANES))
        l_ref[grp] = jnp.broadcast_to(l_new, (8, LANES))
        return alpha, p.astype(BF16)

    qdb = qd.astype(BF16)
    s = jnp.concatenate(
        [jnp.where(own_d, lax.dot_general(qdb, kd_refs[i][0].astype(BF16), nt, preferred_element_type=F32), NEG_BIG)
         for i in range(pps)], axis=1)
    alpha, p = online(0, s)
    w = GROUP_COLS
    pv = _dot(p[:, 0:w], vd_refs[0][0].astype(BF16))
    for i in range(1, pps):
        pv = pv + _dot(p[:, i * w:(i + 1) * w], vd_refs[i][0].astype(BF16))
    accd_ref[...] = alpha * accd_ref[...] + pv

    qfb = qf.astype(BF16)
    s = jnp.concatenate([_dot(qfb, kf_refs[i][0].astype(BF16)) + bias_ref[0, i] for i in range(pps)], axis=1)
    alpha, p = online(1, s)
    w = PAGE_SIZE
    pv = lax.dot_general(p[:, 0:w], vf_refs[0][0].astype(BF16), nt, preferred_element_type=F32)
    for i in range(1, pps):
        pv = pv + lax.dot_general(p[:, i * w:(i + 1) * w], vf_refs[i][0].astype(BF16), nt,
                                  preferred_element_type=F32)
    accf_ref[...] = alpha * accf_ref[...] + pv

    @pl.when(c == n_chunks - 1)
    def _():
        def finish(grp, q, k_new, v_new, acc):
            s = jnp.sum(q * k_new, axis=1, keepdims=True)
            m_old = m_ref[grp][:, 0:1]
            m_new = jnp.maximum(m_old, s)
            alpha = jnp.exp(m_old - m_new)
            p = jnp.exp(s - m_new)
            l_new = alpha * l_ref[grp][:, 0:1] + p
            return (alpha * acc + p * v_new) * (1.0 / l_new)

        twice = lambda a: jnp.concatenate([a, a], axis=0)
        od = finish(0, qd, twice(kan_ref[0]), twice(van_ref[0]), accd_ref[...])
        oa_ref[0] = od[0:DA_HEADS] - _lambda_value(lamv_ref) * od[DA_HEADS:]
        of = finish(1, qf, kfn_ref[0], vfn_ref[0], accf_ref[...])
        of_ref[0] = jnp.sum(jnp.where(own_f, of, 0.0), axis=0, keepdims=True)


def _decode_kernel(pt_ref, *refs, pps):
    del pt_ref
    n_in = 4 * pps + 8
    _decode_step(pl.program_id(1), pl.num_programs(1), refs[:n_in], *refs[n_in:], pps)


def _decode_attention(page_table, cdk, cdv, cfk, cfv, bias, qa, qf, ka_new, va_new, kf_new, vf_new, lamv):
    n, n_pages = page_table.shape
    pps = PAGES_PER_STEP
    steps = n_pages // pps

    def page_spec(i):
        return pl.BlockSpec((1, GROUP_COLS, PAGE_SIZE), lambda b, c, pt: (pt[b, c * pps + i], 0, 0))

    row_spec = pl.BlockSpec((1, 1, GROUP_COLS), lambda b, c, pt: (b, 0, 0))
    head_spec = pl.BlockSpec((1, DA_HEADS, LANES), lambda b, c, pt: (b, 0, 0))
    in_specs = ([page_spec(i) for i in range(pps)] * 4
                + [pl.BlockSpec((1, pps, FX_HEADS, PAGE_SIZE), lambda b, c, pt: (b, c, 0, 0))]
                + [head_spec, row_spec, head_spec, head_spec, row_spec, row_spec]
                + [pl.BlockSpec(lamv.shape, lambda b, c, pt: (0, 0))])
    caches = [cdk] * pps + [cdv] * pps + [cfk] * pps + [cfv] * pps
    return pl.pallas_call(
        functools.partial(_decode_kernel, pps=pps),
        grid_spec=pltpu.PrefetchScalarGridSpec(
            num_scalar_prefetch=1,
            grid=(n, steps),
            in_specs=in_specs,
            out_specs=(head_spec, row_spec),
            scratch_shapes=[pltpu.VMEM((2, 8, LANES), F32), pltpu.VMEM((2, 8, LANES), F32),
                            pltpu.VMEM((8, LANES), F32), pltpu.VMEM((8, GROUP_COLS), F32)],
        ),
        out_shape=(jax.ShapeDtypeStruct((n, DA_HEADS, LANES), F32), jax.ShapeDtypeStruct((n, 1, GROUP_COLS), F32)),
        compiler_params=pltpu.CompilerParams(dimension_semantics=("arbitrary", "arbitrary"),
                                             vmem_limit_bytes=VMEM_LIMIT),
        name="decode_attention",
    )(page_table, *caches, bias, qa, qf, ka_new, va_new, kf_new, vf_new, lamv)


def _post_sample_kernel(x_ref, oa_ref, of_ref, sg_ref, wo_ref, g1_ref, g2_ref, wg_ref, wu_ref, cw_ref, cb_ref,
                        wd_ref, g3_ref, p0_ref, p1_ref, y_ref, gate_ref):
    n_ff = wg_ref.shape[1]
    heads = [_rms(oa_ref[:, h * LANES:(h + 1) * LANES], sg_ref[...]) * (1.0 - LAM_INIT) for h in range(DA_HEADS)]
    o = jnp.concatenate(heads + [of_ref[...]], axis=1).astype(BF16)
    x1 = x_ref[...] + _rms(_dot(o, wo_ref[...]), g1_ref[...])
    h = _rms(x1, g2_ref[...]).astype(BF16)
    out = jnp.zeros_like(x1)
    for c0, width in _ff_chunks(n_ff):
        cols = pl.ds(c0, width)
        g = _dot(h, wg_ref[:, cols])
        gate_ref[:, cols] = g
        u = _dot(h, wu_ref[:, cols])
        out = out + _gated_chunk(u, g, p1_ref[:, cols], p0_ref[:, cols], wd_ref, cw_ref, cb_ref, cols)
    y_ref[...] = x1 + _rms(out, g3_ref[...])


def _post_sample(x, oa, of, sg, wo, g1, g2, wg, wu, cw, cb, wd, g3, p0, p1):
    n, d = x.shape
    return pl.pallas_call(
        _post_sample_kernel,
        out_shape=(jax.ShapeDtypeStruct((n, d), F32), jax.ShapeDtypeStruct((n, wg.shape[1]), F32)),
        compiler_params=pltpu.CompilerParams(vmem_limit_bytes=VMEM_LIMIT),
        name="post_sample",
    )(x, oa, of, sg, wo, g1, g2, wg, wu, cw, cb, wd, g3, p0, p1)


def kernel(x_prompt, x_sample, cache_diff_k, cache_diff_v, cache_fox_k, cache_fox_v, cache_fox_logf, state_ffn_conv, page_table, attn_pre_g, w_in, b_f, lam_q1, lam_k1, lam_q2, lam_k2, subln_g, w_o, attn_post_g, ffn_pre_g, w_gate, w_up, conv_w, conv_b, w_down, ffn_post_g):
    B, S, D = x_prompt.shape
    DB, DS, _ = x_sample.shape
    assert DS == 1 and w_in.shape[0] == 1, "one layer, one new token per sample"
    n_pool = cache_diff_k.shape[1]
    n_pages = page_table.shape[1]
    past_len = n_pages * PAGE_SIZE
    l = 0

    w = w_in[l]
    sec = lambda i: w[:, i * GROUP_COLS:(i + 1) * GROUP_COLS]
    wq = jnp.concatenate([sec(0), sec(3)], axis=1).astype(BF16)
    wk = jnp.concatenate([sec(1), sec(4)], axis=1).astype(BF16)
    wv = jnp.concatenate([sec(2), sec(5)], axis=1).astype(BF16)
    wf = jnp.pad(w[:, 6 * GROUP_COLS:], ((0, 0), (0, LANES - FX_HEADS))).astype(BF16)
    bf = jnp.pad(b_f[l], (0, LANES - FX_HEADS)).reshape(1, LANES)
    wo, wg, wu, wd = (a[l].astype(BF16) for a in (w_o, w_gate, w_up, w_down))
    row = lambda a: a[l].reshape(1, -1)
    g_pre, g_post, g_ffn, g_out, sg, cb = (row(a) for a in (attn_pre_g, attn_post_g, ffn_pre_g, ffn_post_g, subln_g, conv_b))
    cw = conv_w[l]
    half = DA_QK_DIM // 2
    inv = ROPE_THETA ** (-jnp.arange(half, dtype=F32) / half)
    inv = jnp.tile(inv, LANES // half).reshape(1, LANES)
    tri = jnp.tril(jnp.ones((PROJ_T, PROJ_T), BF16))
    lamv = jnp.stack([lam_q1[l], lam_k1[l], lam_q2[l], lam_k2[l]])

    sel = np.zeros((3, LANES, 4 * LANES), np.float32)
    for hd in range(FX_HEADS):
        for term in range(3):
            sel[term, hd, (hd // 2) * LANES + AUX_KEY[hd % 2] + term] = 1.0
    ka4, va4, kfT, vfT, lfT, katt, qT, vT, kx, yq = _proj_prompt(x_prompt, g_pre, wq, wk, wv, wf, bf, inv, tri,
                                                                 jnp.asarray(sel, BF16))
    heads_last = lambda t: jnp.transpose(t.reshape(B, FX_HEADS, FX_DIM, S), (0, 3, 1, 2))
    kf, vf, logf = heads_last(kfT), heads_last(vfT), jnp.transpose(lfT, (0, 2, 1))

    pos = jnp.full((1, LANES), past_len, F32)
    qa_s, qf_s, ka_s, va_s, kf_s, vf_s, lf_s = _proj_sample(x_sample[:, 0, :], g_pre, wq, wk, wv, wf, bf, inv, pos)
    rows_d = lambda c: c[l].reshape(n_pool, PAGE_SIZE * DA_HEADS, 2 * DA_QK_DIM)
    rows_f = lambda c: jnp.transpose(c[l], (0, 2, 3, 1)).reshape(n_pool, GROUP_COLS, PAGE_SIZE)
    lf_pages = jnp.transpose(cache_fox_logf[l], (0, 2, 1))
    key_i = jnp.arange(PAGE_SIZE)
    upper = (key_i[:, None] <= key_i[None, :]).astype(BF16)
    row_i = jnp.arange(n_pages * FX_HEADS)
    later = ((row_i[:, None] % FX_HEADS == row_i[None, :] % FX_HEADS)
             & (row_i[None, :] // FX_HEADS >= row_i[:, None] // FX_HEADS)).astype(BF16)
    lf_new = jnp.broadcast_to(lf_s[:, :FX_HEADS, None], (DB, FX_HEADS, LANES))
    bias = _forget_bias(page_table, lf_pages, lf_new, upper, later).reshape(DB, n_pages, FX_HEADS, PAGE_SIZE)
    r3 = lambda a: a.reshape(DB, 1, GROUP_COLS)
    r4 = lambda a: a.reshape(DB, DA_HEADS, LANES)
    oa_s, of_s = _decode_attention(page_table, rows_d(cache_diff_k), rows_d(cache_diff_v), rows_f(cache_fox_k),
                                   rows_f(cache_fox_v), bias, r4(qa_s), r3(qf_s), r4(ka_s), r4(va_s), r3(kf_s),
                                   r3(vf_s), lamv)
    o = _flash_prompt(qT, katt, kx, vT, yq, lamv, sg.reshape(DA_V_DIM, 1))
    y_prompt, conv_p = _post_prompt(x_prompt, o, wo, g_post, g_ffn, wg, wu, cw, cb, wd, g_out)
    prev = state_ffn_conv[l]
    y_sample, gate_s = _post_sample(x_sample[:, 0, :], oa_s.reshape(DB, GROUP_COLS), of_s[:, 0, :], sg, wo, g_post,
                                    g_ffn, wg, wu, cw, cb, wd, g_out, prev[:, 0, :], prev[:, 1, :])
    conv_s = jnp.stack([prev[:, 1, :], gate_s], axis=1)

    lead = lambda a, shape: a.reshape((1,) + shape)
    return (
        y_prompt,
        y_sample.reshape(DB, DS, D),
        lead(ka4, (B, S, DA_HEADS, 2 * DA_QK_DIM)),
        lead(va4, (B, S, DA_HEADS, DA_V_DIM)),
        lead(kf, (B, S, FX_HEADS, FX_DIM)),
        lead(vf, (B, S, FX_HEADS, FX_DIM)),
        lead(logf, (B, S, FX_HEADS)),
        lead(conv_p, (B, CONV_W - 1, conv_p.shape[-1])),
        lead(ka_s, (DB, DS, DA_HEADS, 2 * DA_QK_DIM)),
        lead(va_s, (DB, DS, DA_HEADS, DA_V_DIM)),
        lead(kf_s, (DB, DS, FX_HEADS, FX_DIM)),
        lead(vf_s, (DB, DS, FX_HEADS, FX_DIM)),
        lead(lf_s[:, :FX_HEADS], (DB, DS, FX_HEADS)),
        lead(conv_s, (DB, CONV_W - 1, conv_s.shape[-1])),
    )
```
